```python
import math
import jax
import jax.numpy as jnp
from jax import lax
import numpy as np

D_MODEL = 2048
BATCH = 4
SEQ = 2048
DEPTH = 4

N_MIXERS = 2
N_ATT_LAYERS = (DEPTH + 1) // 2
N_REC_LAYERS = DEPTH // 2
NORM_EPS = 1e-6
MASK_VALUE = -1e30
LB_FLOOR = 1e-30

DILATED_GROUPS = ((128, 1), (512, 4), (2048, 16))
N_GROUPS = len(DILATED_GROUPS)
ATT_HEAD_DIM = 128
ATT_HEADS_PER_GROUP = D_MODEL // 256
ATT_BLOCK = 128
ROPE_THETA = 500000.0
ROPE_DIM = ATT_HEAD_DIM // 4
ATT_IN_WIDTH = N_GROUPS * 3 * ATT_HEADS_PER_GROUP * ATT_HEAD_DIM
ATT_OUT_WIDTH = ATT_HEADS_PER_GROUP * ATT_HEAD_DIM

HGRN_EXPAND = 128
HGRN_HEADS = D_MODEL // HGRN_EXPAND
HGRN_DK = HGRN_EXPAND
HGRN_DV = D_MODEL // HGRN_HEADS
HGRN_CHUNK = 16
HGRN_IN_WIDTH = 4 * D_MODEL

D_FF = 4 * D_MODEL

kernel_name = "hybrid_dilated_attn_hgrn2_trunk"


def rms_norm(x, g):
    xf = x.astype(jnp.float32)
    xf = xf * lax.rsqrt(jnp.mean(xf * xf, axis=-1, keepdims=True) + NORM_EPS)
    return xf.astype(x.dtype) * g


def partial_rope(x, positions):
    half = ROPE_DIM // 2
    inv_freq = ROPE_THETA ** (-jnp.arange(half, dtype=jnp.float32) / half)
    ang = positions.astype(jnp.float32)[..., None] * inv_freq
    cos = jnp.cos(ang)[:, :, None, :]
    sin = jnp.sin(ang)[:, :, None, :]
    xr = x[..., :ROPE_DIM].astype(jnp.float32)
    x1, x2 = xr[..., :half], xr[..., half:]
    rot = jnp.concatenate([x1 * cos - x2 * sin, x2 * cos + x1 * sin], axis=-1).astype(x.dtype)
    return jnp.concatenate([rot, x[..., ROPE_DIM:]], axis=-1)


def strided_window_attention(q, k, v, dilation, steps):
    B, S, H, dh = q.shape
    L = S // dilation
    N = B * dilation
    nb = -(-L // ATT_BLOCK)
    pad = nb * ATT_BLOCK - L

    def to_blocks(t):
        t = t.reshape(B, L, dilation, H, dh).transpose(0, 2, 1, 3, 4).reshape(N, L, H, dh)
        t = jnp.pad(t, ((0, 0), (0, pad), (0, 0), (0, 0)))
        return t.reshape(N, nb, ATT_BLOCK, H, dh)

    def with_prev(t):
        prev = jnp.concatenate([jnp.zeros_like(t[:, :1]), t[:, :-1]], axis=1)
        return jnp.concatenate([prev, t], axis=2)

    qb = to_blocks(q)
    kk = with_prev(to_blocks(k))
    vv = with_prev(to_blocks(v))

    scores = jnp.einsum('nbqhd,nbkhd->nbhqk', qb, kk).astype(jnp.float32) * (dh ** -0.5)
    blk = jnp.arange(nb)[:, None, None]
    qi = jnp.arange(ATT_BLOCK)[None, :, None] + ATT_BLOCK
    kj = jnp.arange(2 * ATT_BLOCK)[None, None, :]
    dist = qi - kj
    mask = (dist >= 0) & (dist <= steps) & ((blk > 0) | (kj >= ATT_BLOCK))
    scores = jnp.where(mask[None, :, None], scores, MASK_VALUE)
    m = jnp.max(scores, axis=-1, keepdims=True)
    p = jnp.exp(scores - m)
    l = jnp.sum(p, axis=-1, keepdims=True)
    out = jnp.einsum('nbhqk,nbkhd->nbqhd', (p / l).astype(v.dtype), vv)
    lse = (m + jnp.log(l))[..., 0]

    out = out.reshape(N, nb * ATT_BLOCK, H, dh)[:, :L]
    out = out.reshape(B, dilation, L, H, dh).transpose(0, 2, 1, 3, 4).reshape(B, S, H, dh)
    lse = lse.transpose(0, 1, 3, 2).reshape(N, nb * ATT_BLOCK, H)[:, :L]
    lse = lse.reshape(B, dilation, L, H).transpose(0, 2, 1, 3).reshape(B, S, H)
    return out, lse


def dilated_attention_mixer(u, positions, w_in, w_out):
    B, S, _ = u.shape
    proj = (u @ w_in).reshape(B, S, N_GROUPS, 3, ATT_HEADS_PER_GROUP, ATT_HEAD_DIM)
    outs, lses = [], []
    for g, (window, dilation) in enumerate(DILATED_GROUPS):
        q = partial_rope(proj[:, :, g, 0], positions)
        k = partial_rope(proj[:, :, g, 1], positions)
        v = proj[:, :, g, 2]
        o, lse = strided_window_attention(q, k, v, dilation, window // dilation)
        outs.append(o)
        lses.append(lse)
    w = jax.nn.softmax(jnp.stack(lses, axis=0), axis=0)
    o = jnp.sum(w[..., None] * jnp.stack(outs, axis=0).astype(jnp.float32), axis=0)
    return o.astype(u.dtype).reshape(B, S, ATT_OUT_WIDTH) @ w_out


def hgrn2_mixer(u, lower_bound, out_norm_g, w_in, w_out):
    B, S, _ = u.shape
    H, C = HGRN_HEADS, HGRN_CHUNK
    nC = S // C
    proj = (u @ w_in).reshape(B, S, 4, H, HGRN_DK)
    q, f_pre, i_val, gate = proj[:, :, 0], proj[:, :, 1], proj[:, :, 2], proj[:, :, 3]
    lb = lower_bound.astype(jnp.float32).reshape(H, HGRN_DK)
    log_f = jnp.logaddexp(jnp.log(jnp.maximum(lb, LB_FLOOR)),
                          jnp.log1p(-lb) + jax.nn.log_sigmoid(f_pre.astype(jnp.float32)))
    k = -jnp.expm1(log_f)

    def chunks(t):
        return t.astype(jnp.float32).reshape(B, nC, C, H, t.shape[-1]).transpose(1, 0, 3, 2, 4)

    causal = jnp.tril(jnp.ones((C, C), dtype=bool))[None, None, :, :, None]

    def step(state, inp):
        q_c, k_c, v_c, g_c = inp
        b = jnp.cumsum(g_c, axis=2)
        inter = jnp.einsum('bhtk,bhkv->bhtv', q_c * jnp.exp(b), state)
        diff = b[:, :, :, None, :] - b[:, :, None, :, :]
        decay = jnp.exp(jnp.where(causal, diff, MASK_VALUE))
        a = jnp.einsum('bhtk,bhsk,bhtsk->bhts', q_c, k_c, decay)
        intra = jnp.einsum('bhts,bhsv->bhtv', a, v_c)
        b_last = b[:, :, -1:, :]
        new_state = jnp.exp(b_last[:, :, 0, :])[..., None] * state + jnp.einsum(
            'bhsk,bhsv->bhkv', k_c * jnp.exp(b_last - b), v_c)
        return new_state, inter + intra

    state0 = jnp.zeros((B, H, HGRN_DK, HGRN_DV), jnp.float32)
    _, o = lax.scan(step, state0, (chunks(q), chunks(k), chunks(i_val), chunks(log_f)))
    o = o.transpose(1, 0, 3, 2, 4).reshape(B, S, H, HGRN_DV)
    o = rms_norm(o, out_norm_g) * jax.nn.silu(gate.astype(jnp.float32))
    return o.reshape(B, S, H * HGRN_DV).astype(u.dtype) @ w_out


def squared_relu_mlp(u, w1, w2):
    return jnp.square(jax.nn.relu(u @ w1)) @ w2


def setup_inputs(seed: int = 0) -> dict:
    key = jax.random.key(seed)
    ks = jax.random.split(key, 12)
    f32 = jnp.float32
    x = jax.random.normal(ks[0], (BATCH, SEQ, D_MODEL), f32)
    offsets = jax.random.randint(ks[1], (BATCH, 1), 0, 4096, dtype=jnp.int32)
    positions = offsets + jnp.arange(SEQ, dtype=jnp.int32)[None, :]
    norm_gains = 1.0 + 0.02 * jax.random.normal(ks[2], (DEPTH, 4, D_MODEL), f32)
    w_att_in = jax.random.normal(ks[3], (N_ATT_LAYERS, D_MODEL, ATT_IN_WIDTH), f32) * D_MODEL ** -0.5
    w_att_out = jax.random.normal(ks[4], (N_ATT_LAYERS, ATT_OUT_WIDTH, D_MODEL), f32) * ATT_OUT_WIDTH ** -0.5
    w_rec_in = jax.random.normal(ks[5], (N_REC_LAYERS, D_MODEL, HGRN_IN_WIDTH), f32) * D_MODEL ** -0.5
    rec_lower_bounds = 0.5 * jax.random.normal(ks[6], (N_REC_LAYERS, HGRN_HEADS * HGRN_DK), f32)
    rec_out_norm = 1.0 + 0.02 * jax.random.normal(ks[7], (N_REC_LAYERS, HGRN_DV), f32)
    w_rec_out = jax.random.normal(ks[8], (N_REC_LAYERS, HGRN_HEADS * HGRN_DV, D_MODEL), f32) * D_MODEL ** -0.5
    w_ff1 = jax.random.normal(ks[9], (DEPTH, D_MODEL, D_FF), f32) * D_MODEL ** -0.5
    w_ff2 = jax.random.normal(ks[10], (DEPTH, D_FF, D_MODEL), f32) * D_FF ** -0.5
    return {"x": x, "positions": positions, "norm_gains": norm_gains,
            "w_att_in": w_att_in, "w_att_out": w_att_out,
            "w_rec_in": w_rec_in, "rec_lower_bounds": rec_lower_bounds,
            "rec_out_norm": rec_out_norm, "w_rec_out": w_rec_out,
            "w_ff1": w_ff1, "w_ff2": w_ff2}


def reference(x, positions, norm_gains, w_att_in, w_att_out, w_rec_in, rec_lower_bounds,
              rec_out_norm, w_rec_out, w_ff1, w_ff2):
    lb = jax.nn.softmax(rec_lower_bounds.astype(jnp.float32), axis=0)
    lb = jnp.cumsum(lb, axis=0) - lb[0:1]
    h = x
    for layer in range(DEPTH):
        g = norm_gains[layer]
        u = rms_norm(h, g[0])
        j = layer // N_MIXERS
        if layer % N_MIXERS == 0:
            y = dilated_attention_mixer(u, positions, w_att_in[j], w_att_out[j])
        else:
            y = hgrn2_mixer(u, lb[j], rec_out_norm[j], w_rec_in[j], w_rec_out[j])
        h = h + rms_norm(y, g[1])
        u = rms_norm(h, g[2])
        h = h + rms_norm(squared_relu_mlp(u, w_ff1[layer], w_ff2[layer]), g[3])
    return h
```

```python
import functools
import math

import numpy as np
import jax
import jax.numpy as jnp
from jax import lax
from jax.experimental import pallas as pl
from jax.experimental.pallas import tpu as pltpu

F32 = jnp.float32
BF16 = jnp.bfloat16

NORM_EPS = 1e-6
MASK_VALUE = -1e30
LB_FLOOR = 1e-30

DILATED_GROUPS = ((128, 1), (512, 4), (2048, 16))
N_GROUPS = len(DILATED_GROUPS)
HEAD_DIM = 128
ATT_HEADS = 8
ATT_BLOCK = 128
ATT_GROUP_WIDTH = ATT_HEADS * HEAD_DIM
ROPE_THETA = 500000.0
ROPE_DIM = HEAD_DIM // 4
ROPE_HALF = ROPE_DIM // 2
HGRN_CHUNK = 128
HGRN_LEVELS = 7
HGRN_HEADS_PER_STEP = 2

VMEM_LIMIT_BYTES = 56 * 1024 * 1024


def _params(semantics):
    return pltpu.CompilerParams(dimension_semantics=semantics, vmem_limit_bytes=VMEM_LIMIT_BYTES)


def _rms_scale(x, gain):
    ms = jnp.mean(x * x, axis=-1, keepdims=True)
    return x * lax.rsqrt(ms + NORM_EPS) * gain


def _rope_kernel(pos_ref, invf_ref, cos_ref, sa_ref, sb_ref):
    ang = pos_ref[...].astype(F32) * invf_ref[...]
    lane = lax.broadcasted_iota(jnp.int32, ang.shape, 1)
    c = jnp.cos(ang)
    s = jnp.sin(ang)
    cos_ref[...] = jnp.where(lane < ROPE_DIM, c, 1.0)
    sa_ref[...] = jnp.where((lane >= ROPE_HALF) & (lane < ROPE_DIM), s, 0.0)
    sb_ref[...] = jnp.where(lane < ROPE_HALF, -s, 0.0)


def _rope_tables(positions):
    t = positions.size
    tr = 1024
    inv_freq = ROPE_THETA ** (-jnp.arange(ROPE_HALF, dtype=F32) / ROPE_HALF)
    invf = jnp.zeros((1, HEAD_DIM), F32).at[0, :ROPE_DIM].set(jnp.tile(inv_freq, 2))
    out = jax.ShapeDtypeStruct((t, HEAD_DIM), F32)
    return pl.pallas_call(
        _rope_kernel,
        grid=(t // tr,),
        in_specs=[pl.BlockSpec((tr, 1), lambda i: (i, 0)),
                  pl.BlockSpec((1, HEAD_DIM), lambda i: (0, 0))],
        out_specs=[pl.BlockSpec((tr, HEAD_DIM), lambda i: (i, 0))] * 3,
        out_shape=[out, out, out],
        compiler_params=_params(("parallel",)),
        name="rope_tables",
    )(positions.reshape(t, 1), invf)


def _in_proj_kernel(*refs, rope):
    if rope:
        x_ref, g_ref, w_ref, cos_ref, sa_ref, sb_ref, o_ref, xn_ref = refs
    else:
        x_ref, g_ref, w_ref, o_ref, xn_ref = refs

    @pl.when(pl.program_id(3) == 0)
    def _():
        xn_ref[...] = _rms_scale(x_ref[...], g_ref[...]).astype(BF16)

    acc = jnp.dot(xn_ref[...], w_ref[...], preferred_element_type=F32)
    if not rope:
        o_ref[...] = acc.astype(o_ref.dtype)
        return
    cos, sa, sb = cos_ref[...], sa_ref[...], sb_ref[...]
    scale = HEAD_DIM ** -0.5
    for part, mult in ((0, scale), (1, 1.0)):
        c, a, b = cos * mult, sa * mult, sb * mult
        for hd in range(ATT_HEADS):
            lo = part * ATT_GROUP_WIDTH + hd * HEAD_DIM
            xh = acc[:, lo:lo + HEAD_DIM]
            r = (xh * c + pltpu.roll(xh, ROPE_HALF, 1) * a
                 + pltpu.roll(xh, HEAD_DIM - ROPE_HALF, 1) * b)
            o_ref[:, lo:lo + HEAD_DIM] = r.astype(o_ref.dtype)
    o_ref[:, 2 * ATT_GROUP_WIDTH:] = acc[:, 2 * ATT_GROUP_WIDTH:].astype(o_ref.dtype)


def _in_proj(h, gain, w, col0, width, tm, tn, dilation, batch, seq, out_dtype, tables=None):
    t, d_model = h.shape
    length = seq // dilation
    tm = min(tm, length)
    rope = tables is not None
    x = h.reshape(batch, length, dilation * d_model)
    grid = (batch, dilation, length // tm, width // tn)
    in_specs = [
        pl.BlockSpec((None, tm, d_model), lambda b, r, i, n: (b, i, r)),
        pl.BlockSpec((1, d_model), lambda b, r, i, n: (0, 0)),
        pl.BlockSpec((d_model, tn), lambda b, r, i, n: (0, col0 // tn + n)),
    ]
    args = [x, gain.reshape(1, d_model), w]
    if rope:
        for tab in tables:
            in_specs.append(pl.BlockSpec((None, tm, HEAD_DIM), lambda b, r, i, n: (b, i, r)))
            args.append(tab.reshape(batch, length, dilation * HEAD_DIM))
    return pl.pallas_call(
        functools.partial(_in_proj_kernel, rope=rope),
        grid=grid,
        in_specs=in_specs,
        out_specs=pl.BlockSpec((None, None, tm, tn), lambda b, r, i, n: (b, r, i, n)),
        out_shape=jax.ShapeDtypeStruct((batch, dilation, length, width), out_dtype),
        scratch_shapes=[pltpu.VMEM((tm, d_model), BF16)],
        compiler_params=_params(("parallel", "parallel", "parallel", "arbitrary")),
        name="in_proj_rope" if rope else "in_proj",
    )(*args)


def _attention_kernel(*refs, has_prev):
    if has_prev:
        q_ref, kc_ref, kp_ref, vc_ref, vp_ref, o_ref, lse_ref = refs
    else:
        q_ref, kc_ref, vc_ref, o_ref, lse_ref = refs
    blk = pl.program_id(2)
    row = lax.broadcasted_iota(jnp.int32, (ATT_BLOCK, ATT_BLOCK), 0)
    col = lax.broadcasted_iota(jnp.int32, (ATT_BLOCK, ATT_BLOCK), 1)
    mask_c = col <= row
    if has_prev:
        mask_p = (col >= row) & (blk > 0)
    nt = (((1,), (1,)), ((), ()))
    lane = lax.broadcasted_iota(jnp.int32, (ATT_BLOCK, HEAD_DIM), 1)
    lse_tile = jnp.zeros((ATT_BLOCK, HEAD_DIM), F32)
    for hd in range(ATT_HEADS):
        sl = slice(hd * HEAD_DIM, (hd + 1) * HEAD_DIM)
        q = q_ref[:, sl]
        s_c = lax.dot_general(q, kc_ref[:, sl], nt, preferred_element_type=F32)
        s_c = jnp.where(mask_c, s_c, MASK_VALUE)
        m = jnp.max(s_c, axis=-1, keepdims=True)
        if has_prev:
            s_p = lax.dot_general(q, kp_ref[:, sl], nt, preferred_element_type=F32)
            s_p = jnp.where(mask_p, s_p, MASK_VALUE)
            m = jnp.maximum(m, jnp.max(s_p, axis=-1, keepdims=True))
        p_c = jnp.exp(s_c - m)
        l = jnp.sum(p_c, axis=-1, keepdims=True)
        acc = jnp.dot(p_c.astype(BF16), vc_ref[:, sl], preferred_element_type=F32)
        if has_prev:
            p_p = jnp.exp(s_p - m)
            l = l + jnp.sum(p_p, axis=-1, keepdims=True)
            acc = acc + jnp.dot(p_p.astype(BF16), vp_ref[:, sl], preferred_element_type=F32)
        o_ref[:, sl] = (acc / l).astype(o_ref.dtype)
        lse_tile = jnp.where(lane == hd, m + jnp.log(l), lse_tile)
    lse_ref[...] = lse_tile


def _attention(qkv, dilation, batch, seq):
    length = seq // dilation
    nb = length // ATT_BLOCK
    has_prev = nb > 1
    blk_shape = (None, None, ATT_BLOCK, ATT_GROUP_WIDTH)

    def cur(part):
        return pl.BlockSpec(blk_shape, lambda b, r, i: (b, r, i, part))

    def prev(part):
        return pl.BlockSpec(blk_shape, lambda b, r, i: (b, r, jnp.maximum(i - 1, 0), part))

    if has_prev:
        in_specs = [cur(0), cur(1), prev(1), cur(2), prev(2)]
    else:
        in_specs = [cur(0), cur(1), cur(2)]
    o, lse = pl.pallas_call(
        functools.partial(_attention_kernel, has_prev=has_prev),
        grid=(batch, dilation, nb),
        in_specs=in_specs,
        out_specs=[pl.BlockSpec((None, ATT_BLOCK, ATT_GROUP_WIDTH), lambda b, r, i: (b, i, r)),
                   pl.BlockSpec((None, ATT_BLOCK, HEAD_DIM), lambda b, r, i: (b, i, r))],
        out_shape=[jax.ShapeDtypeStruct((batch, length, dilation * ATT_GROUP_WIDTH), BF16),
                   jax.ShapeDtypeStruct((batch, length, dilation * HEAD_DIM), F32)],
        compiler_params=_params(("parallel", "parallel", "arbitrary")),
        name="banded_attention",
    )(*([qkv] * len(in_specs)))
    return o.reshape(batch * seq, ATT_GROUP_WIDTH), lse.reshape(batch * seq, HEAD_DIM)


def _out_proj_kernel(*refs, merge):
    if merge:
        o1, o2, o3, l1, l2, l3, h_ref, g_ref, w_ref, out_ref, a_ref = refs
        lses = (l1[...], l2[...], l3[...])
        m = jnp.maximum(jnp.maximum(lses[0], lses[1]), lses[2])
        es = [jnp.exp(v - m) for v in lses]
        den = es[0] + es[1] + es[2]
        ws = [e / den for e in es]
        for hd in range(ATT_HEADS):
            sl = slice(hd * HEAD_DIM, (hd + 1) * HEAD_DIM)
            acc = ws[0][:, hd:hd + 1] * o1[:, sl].astype(F32)
            acc = acc + ws[1][:, hd:hd + 1] * o2[:, sl].astype(F32)
            acc = acc + ws[2][:, hd:hd + 1] * o3[:, sl].astype(F32)
            a_ref[:, sl] = acc.astype(BF16)
        a = a_ref[...]
    else:
        a_in, h_ref, g_ref, w_ref, out_ref = refs
        a = a_in[...]
    y = jnp.dot(a, w_ref[...], preferred_element_type=F32)
    out_ref[...] = h_ref[...] + _rms_scale(y, g_ref[...])


def _out_proj(h, gain, w, mixed, lses=None):
    t, d_model = h.shape
    merge = lses is not None
    k = w.shape[0]
    tm = 256
    row = lambda i: (i, 0)
    fixed = lambda i: (0, 0)
    if merge:
        in_specs = [pl.BlockSpec((tm, ATT_GROUP_WIDTH), row)] * 3 + [pl.BlockSpec((tm, HEAD_DIM), row)] * 3
        args = list(mixed) + list(lses)
        scratch = [pltpu.VMEM((tm, k), BF16)]
    else:
        in_specs = [pl.BlockSpec((tm, k), row)]
        args = [mixed]
        scratch = []
    in_specs += [pl.BlockSpec((tm, d_model), row), pl.BlockSpec((1, d_model), fixed),
                 pl.BlockSpec((k, d_model), fixed)]
    args += [h, gain.reshape(1, d_model), w]
    return pl.pallas_call(
        functools.partial(_out_proj_kernel, merge=merge),
        grid=(t // tm,),
        in_specs=in_specs,
        out_specs=pl.BlockSpec((tm, d_model), row),
        out_shape=jax.ShapeDtypeStruct((t, d_model), F32),
        scratch_shapes=scratch,
        compiler_params=_params(("parallel",)),
        name="merge_out_proj" if merge else "out_proj",
    )(*args)


def _mlp_kernel(h_ref, g_in_ref, g_out_ref, w1_ref, w2_ref, out_ref, un_ref, acc_ref):
    f = pl.program_id(1)

    @pl.when(f == 0)
    def _():
        un_ref[...] = _rms_scale(h_ref[...], g_in_ref[...]).astype(BF16)
        acc_ref[...] = jnp.zeros_like(acc_ref)

    a = jnp.dot(un_ref[...], w1_ref[...], preferred_element_type=F32)
    a = jnp.square(jnp.maximum(a, 0.0))
    acc_ref[...] += jnp.dot(a.astype(BF16), w2_ref[...], preferred_element_type=F32)

    @pl.when(f == pl.num_programs(1) - 1)
    def _():
        out_ref[...] = h_ref[...] + _rms_scale(acc_ref[...], g_out_ref[...])


def _mlp(h, g_in, g_out, w1, w2):
    t, d_model = h.shape
    d_ff = w1.shape[1]
    tm, tf = 512, 512
    return pl.pallas_call(
        _mlp_kernel,
        grid=(t // tm, d_ff // tf),
        in_specs=[pl.BlockSpec((tm, d_model), lambda m, f: (m, 0)),
                  pl.BlockSpec((1, d_model), lambda m, f: (0, 0)),
                  pl.BlockSpec((1, d_model), lambda m, f: (0, 0)),
                  pl.BlockSpec((d_model, tf), lambda m, f: (0, f)),
                  pl.BlockSpec((tf, d_model), lambda m, f: (f, 0))],
        out_specs=pl.BlockSpec((tm, d_model), lambda m, f: (m, 0)),
        out_shape=jax.ShapeDtypeStruct((t, d_model), F32),
        scratch_shapes=[pltpu.VMEM((tm, d_model), BF16), pltpu.VMEM((tm, d_model), F32)],
        compiler_params=_params(("parallel", "arbitrary")),
        name="mlp",
    )(h, g_in.reshape(1, d_model), g_out.reshape(1, d_model), w1, w2)


def _hgrn_constants():
    c_sz = HGRN_CHUNK
    r = np.arange(c_sz)[:, None]
    c = np.arange(c_sz)[None, :]
    mats = [c <= r, c > r]
    for j in range(HGRN_LEVELS):
        half = 1 << j
        ref = (r // (2 * half)) * (2 * half) + half - 1
        upper = (r > ref) & (c > ref) & (c <= r)
        lower = (r <= ref) & (c > r) & (c <= ref)
        mats.append(upper | lower)
    mstack = np.concatenate(mats, axis=0).astype(np.float32)
    x = r ^ c
    msb = np.floor(np.log2(np.maximum(x, 1))).astype(np.int32)
    lvl = np.where(c < r, msb, -1).astype(np.int32)
    return mstack, lvl


def _hgrn_kernel(q_ref, f_ref, i_ref, gt_ref, lbp_ref, gn_ref, m_ref, lvl_ref, o_ref, st_ref, *, layer_j):
    nh = HGRN_HEADS_PER_STEP
    cs = HGRN_CHUNK

    @pl.when(pl.program_id(2) == 0)
    def _():
        st_ref[...] = jnp.zeros_like(st_ref)

    lbp = lbp_ref[...]
    e = jnp.exp(lbp - jnp.max(lbp, axis=0, keepdims=True))
    p = e / jnp.sum(e, axis=0, keepdims=True)
    csum = p[0:1]
    for t in range(1, layer_j + 1):
        csum = csum + p[t:t + 1]
    lb = csum - p[0:1]
    a_row = jnp.log(jnp.maximum(lb, LB_FLOOR))
    c_row = jnp.log1p(-lb)

    z = f_ref[...]
    en = jnp.exp(-jnp.abs(z))
    log_sig = jnp.minimum(z, 0.0) - jnp.log1p(en)
    cc = c_row + log_sig
    log_f = jnp.maximum(a_row, cc) + jnp.log1p(jnp.exp(-jnp.abs(a_row - cc)))
    sig_neg = jnp.where(z >= 0.0, en, 1.0) / (1.0 + en)
    kk = (1.0 - lb) * sig_neg - (jnp.maximum(lb, LB_FLOOR) - lb)

    g_hi = log_f.astype(BF16)
    r1 = log_f - g_hi.astype(F32)
    g_mid = r1.astype(BF16)
    g_lo = (r1 - g_mid.astype(F32)).astype(BF16)
    msk = m_ref[...]
    dall = (jnp.dot(msk, g_hi, preferred_element_type=F32)
            + jnp.dot(msk, g_mid, preferred_element_type=F32)
            + jnp.dot(msk, g_lo, preferred_element_type=F32))

    lvl = lvl_ref[...]
    nt = (((1,), (1,)), ((), ()))
    tn = (((0,), (0,)), ((), ()))
    for hh in range(nh):
        sl = slice(hh * HEAD_DIM, (hh + 1) * HEAD_DIM)
        qh = q_ref[:, sl]
        kh = kk[:, sl]
        vh = i_ref[:, sl]
        vb = vh.astype(BF16)
        a_mat = jnp.zeros((cs, cs), F32)
        for j in range(HGRN_LEVELS):
            ej = jnp.exp(dall[(2 + j) * cs:(3 + j) * cs, sl])
            aj = lax.dot_general((qh * ej).astype(BF16), (kh * ej).astype(BF16), nt,
                                 preferred_element_type=F32)
            a_mat = jnp.where(lvl == j, aj, a_mat)
        b_incl = dall[0:cs, sl]
        q_dec = (qh * jnp.exp(b_incl)).astype(BF16)
        k_dec = (kh * jnp.exp(dall[cs:2 * cs, sl])).astype(BF16)
        st = st_ref[hh]
        inter = lax.dot_general(q_dec, st.astype(BF16), nt, preferred_element_type=F32)
        intra = jnp.dot(a_mat.astype(BF16), vb, preferred_element_type=F32)
        diag = jnp.sum(qh * kh, axis=-1, keepdims=True) * vh
        o = inter + intra + diag
        decay_all = jnp.exp(b_incl[cs - 1:cs, :])
        st_ref[hh] = st * decay_all + lax.dot_general(vb, k_dec, tn, preferred_element_type=F32)
        gt = gt_ref[:, sl]
        on = _rms_scale(o, gn_ref[...]) * (gt * (1.0 / (1.0 + jnp.exp(-gt))))
        o_ref[:, sl] = on.astype(o_ref.dtype)


def _hgrn_recurrence(proj, lower_bound_params, layer_j, out_norm_gain, batch, seq, d_model):
    nh = HGRN_HEADS_PER_STEP
    wb = nh * HEAD_DIM
    hb = d_model // wb
    mstack, lvl = _hgrn_constants()
    n_layers = lower_bound_params.shape[0]

    def part(k):
        return pl.BlockSpec((None, HGRN_CHUNK, wb), lambda b, g, c: (b, c, k * hb + g))

    fixed = lambda b, g, c: (0, 0)
    out = pl.pallas_call(
        functools.partial(_hgrn_kernel, layer_j=layer_j),
        grid=(batch, hb, seq // HGRN_CHUNK),
        in_specs=[part(0), part(1), part(2), part(3),
                  pl.BlockSpec((n_layers, wb), lambda b, g, c: (0, g)),
                  pl.BlockSpec((1, HEAD_DIM), fixed),
                  pl.BlockSpec(mstack.shape, fixed),
                  pl.BlockSpec(lvl.shape, fixed)],
        out_specs=pl.BlockSpec((None, HGRN_CHUNK, wb), lambda b, g, c: (b, c, g)),
        out_shape=jax.ShapeDtypeStruct((batch, seq, d_model), BF16),
        scratch_shapes=[pltpu.VMEM((nh, HEAD_DIM, HEAD_DIM), F32)],
        compiler_params=_params(("parallel", "parallel", "arbitrary")),
        name="hgrn_recurrence",
    )(proj, proj, proj, proj, lower_bound_params, out_norm_gain.reshape(1, HEAD_DIM),
      jnp.asarray(mstack, BF16), jnp.asarray(lvl))
    return out.reshape(batch * seq, d_model)


@jax.jit
def kernel(x, positions, norm_gains, w_att_in, w_att_out, w_rec_in, rec_lower_bounds, rec_out_norm,
           w_rec_out, w_ff1, w_ff2):
    batch, seq, d_model = x.shape
    depth = norm_gains.shape[0]
    tables = _rope_tables(positions)
    h = x.reshape(batch * seq, d_model)
    for layer in range(depth):
        g = norm_gains[layer]
        j = layer // 2
        if layer % 2 == 0:
            w_in = w_att_in[j].astype(BF16)
            group_w = 3 * ATT_GROUP_WIDTH
            outs, lses = [], []
            for gi, (window, dilation) in enumerate(DILATED_GROUPS):
                assert window // dilation == ATT_BLOCK
                qkv = _in_proj(h, g[0], w_in, gi * group_w, group_w, 256, group_w, dilation, batch, seq,
                               BF16, tables)
                o, lse = _attention(qkv, dilation, batch, seq)
                outs.append(o)
                lses.append(lse)
            h = _out_proj(h, g[1], w_att_out[j].astype(BF16), outs, lses)
        else:
            width = w_rec_in.shape[2]
            proj = _in_proj(h, g[0], w_rec_in[j].astype(BF16), 0, width, 512, 1024, 1, batch, seq, F32)
            mixed = _hgrn_recurrence(proj.reshape(batch, seq, width), rec_lower_bounds, j,
                                     rec_out_norm[j], batch, seq, d_model)
            h = _out_proj(h, g[1], w_rec_out[j].astype(BF16), mixed)
        h = _mlp(h, g[2], g[3], w_ff1[layer].astype(BF16), w_ff2[layer].astype(BF16))
    return h.reshape(batch, seq, d_model)
```

```python
import functools

import numpy as np
import jax
import jax.numpy as jnp
from jax import lax
from jax.experimental import pallas as pl
from jax.experimental.pallas import tpu as pltpu

F32 = jnp.float32
BF16 = jnp.bfloat16

NORM_EPS = 1e-6
MASK_VALUE = -1e30
LB_FLOOR = 1e-30

DILATED_GROUPS = ((128, 1), (512, 4), (2048, 16))
HEAD_DIM = 128
ATT_HEADS = 8
ATT_BLOCK = 128
ATT_GROUP_WIDTH = ATT_HEADS * HEAD_DIM
ROPE_THETA = 500000.0
ROPE_DIM = HEAD_DIM // 4
ROPE_HALF = ROPE_DIM // 2
TOKEN_TILE = 256
HGRN_CHUNK = 128
HGRN_LEVELS = 7
HGRN_HEADS_PER_STEP = 4

VMEM_LIMIT_BYTES = 56 * 1024 * 1024


def _params(semantics):
    return pltpu.CompilerParams(dimension_semantics=semantics, vmem_limit_bytes=VMEM_LIMIT_BYTES)


def _rms_scale(x, gain):
    ms = jnp.mean(x * x, axis=-1, keepdims=True)
    return x * lax.rsqrt(ms + NORM_EPS) * gain


def _rope_kernel(pos_ref, invf_ref, cos_ref, sa_ref, sb_ref):
    ang = pos_ref[...].astype(F32) * invf_ref[...]
    lane = lax.broadcasted_iota(jnp.int32, ang.shape, 1)
    c = jnp.cos(ang)
    s = jnp.sin(ang)
    cos_ref[...] = jnp.where(lane < ROPE_DIM, c, 1.0)
    sa_ref[...] = jnp.where((lane >= ROPE_HALF) & (lane < ROPE_DIM), s, 0.0)
    sb_ref[...] = jnp.where(lane < ROPE_HALF, -s, 0.0)


def _rope_tables(positions, dilation):
    batch, seq = positions.shape
    t = batch * seq
    tr = 1024
    pos = positions.reshape(batch, seq // dilation, dilation).transpose(0, 2, 1)
    inv_freq = ROPE_THETA ** (-jnp.arange(ROPE_HALF, dtype=F32) / ROPE_HALF)
    invf = jnp.zeros((1, HEAD_DIM), F32).at[0, :ROPE_DIM].set(jnp.tile(inv_freq, 2))
    out = jax.ShapeDtypeStruct((t, HEAD_DIM), F32)
    tabs = pl.pallas_call(
        _rope_kernel,
        grid=(t // tr,),
        in_specs=[pl.BlockSpec((tr, 1), lambda i: (i, 0)),
                  pl.BlockSpec((1, HEAD_DIM), lambda i: (0, 0))],
        out_specs=[pl.BlockSpec((tr, HEAD_DIM), lambda i: (i, 0))] * 3,
        out_shape=[out, out, out],
        compiler_params=_params(("parallel",)),
        name="rope_tables",
    )(pos.reshape(t, 1), invf)
    return [tab.reshape(batch, dilation, seq // dilation, HEAD_DIM) for tab in tabs]


def _in_proj_kernel(*refs, rope, dilation):
    refs = list(refs)
    planes_ref = refs.pop() if dilation > 1 else None
    xn_ref = refs.pop()
    o_ref = refs.pop()
    x_ref, g_ref, w_ref = refs[:3]
    tm, d_model = x_ref.shape
    n = tm // dilation

    @pl.when(pl.program_id(2) == 0)
    def _():
        xn = _rms_scale(x_ref[...], g_ref[...])
        if dilation == 1:
            xn_ref[...] = xn.astype(BF16)
            return
        for c in range(d_model // HEAD_DIM):
            planes_ref[c] = xn[:, c * HEAD_DIM:(c + 1) * HEAD_DIM]
        for r in range(dilation):
            for c in range(d_model // HEAD_DIM):
                rows = planes_ref[c, pl.ds(r, n, stride=dilation), :]
                xn_ref[r * n:(r + 1) * n, c * HEAD_DIM:(c + 1) * HEAD_DIM] = rows.astype(BF16)

    acc = jnp.dot(xn_ref[...], w_ref[...], preferred_element_type=F32)

    def store(lo, val):
        for r in range(dilation):
            o_ref[r, :, lo:lo + val.shape[1]] = val[r * n:(r + 1) * n].astype(o_ref.dtype)

    if not rope:
        store(0, acc)
        return
    cos, sa, sb = (ref[...].reshape(tm, HEAD_DIM) for ref in refs[3:6])
    scale = HEAD_DIM ** -0.5
    for part, mult in ((0, scale), (1, 1.0)):
        c, a, b = cos * mult, sa * mult, sb * mult
        for hd in range(ATT_HEADS):
            lo = part * ATT_GROUP_WIDTH + hd * HEAD_DIM
            xh = acc[:, lo:lo + HEAD_DIM]
            rot = (xh * c + pltpu.roll(xh, ROPE_HALF, 1) * a
                   + pltpu.roll(xh, HEAD_DIM - ROPE_HALF, 1) * b)
            store(lo, rot)
    store(2 * ATT_GROUP_WIDTH, acc[:, 2 * ATT_GROUP_WIDTH:])


def _in_proj(h, gain, w, col_block, width, tm, tn, dilation, batch, seq, out_dtype, tables=None):
    t, d_model = h.shape
    rope = tables is not None
    tiles = seq // tm
    n = tm // dilation
    grid = (batch, tiles, width // tn)
    residue_tile = lambda b, i, j: (b, 0, i, 0)
    in_specs = [
        pl.BlockSpec((tm, d_model), lambda b, i, j: (b * tiles + i, 0)),
        pl.BlockSpec((1, d_model), lambda b, i, j: (0, 0)),
        pl.BlockSpec((d_model, tn), lambda b, i, j: (0, col_block(j))),
    ]
    args = [h, gain.reshape(1, d_model), w]
    if rope:
        in_specs += [pl.BlockSpec((None, dilation, n, HEAD_DIM), residue_tile)] * 3
        args += list(tables)
    scratch = [pltpu.VMEM((tm, d_model), BF16)]
    if dilation > 1:
        scratch.append(pltpu.VMEM((d_model // HEAD_DIM, tm, HEAD_DIM), F32))
    return pl.pallas_call(
        functools.partial(_in_proj_kernel, rope=rope, dilation=dilation),
        grid=grid,
        in_specs=in_specs,
        out_specs=pl.BlockSpec((None, dilation, n, tn), lambda b, i, j: (b, 0, i, j)),
        out_shape=jax.ShapeDtypeStruct((batch, dilation, seq // dilation, width), out_dtype),
        scratch_shapes=scratch,
        compiler_params=_params(("parallel", "parallel", "arbitrary")),
        name="in_proj_rope" if rope else "in_proj",
    )(*args)


def _attention_kernel(*refs, has_prev):
    if has_prev:
        q_ref, kc_ref, kp_ref, vc_ref, vp_ref, o_ref, lse_ref = refs
    else:
        q_ref, kc_ref, vc_ref, o_ref, lse_ref = refs
    blk = pl.program_id(2)
    row = lax.broadcasted_iota(jnp.int32, (ATT_BLOCK, ATT_BLOCK), 0)
    col = lax.broadcasted_iota(jnp.int32, (ATT_BLOCK, ATT_BLOCK), 1)
    mask_c = col <= row
    if has_prev:
        mask_p = (col >= row) & (blk > 0)
    nt = (((1,), (1,)), ((), ()))
    lane = lax.broadcasted_iota(jnp.int32, (ATT_BLOCK, HEAD_DIM), 1)
    lse_tile = jnp.zeros((ATT_BLOCK, HEAD_DIM), F32)
    for hd in range(ATT_HEADS):
        sl = slice(hd * HEAD_DIM, (hd + 1) * HEAD_DIM)
        q = q_ref[:, sl]
        s_c = lax.dot_general(q, kc_ref[:, sl], nt, preferred_element_type=F32)
        s_c = jnp.where(mask_c, s_c, MASK_VALUE)
        m = jnp.max(s_c, axis=-1, keepdims=True)
        if has_prev:
            s_p = lax.dot_general(q, kp_ref[:, sl], nt, preferred_element_type=F32)
            s_p = jnp.where(mask_p, s_p, MASK_VALUE)
            m = jnp.maximum(m, jnp.max(s_p, axis=-1, keepdims=True))
        p_c = jnp.exp(s_c - m)
        l = jnp.sum(p_c, axis=-1, keepdims=True)
        acc = jnp.dot(p_c.astype(BF16), vc_ref[:, sl], preferred_element_type=F32)
        if has_prev:
            p_p = jnp.exp(s_p - m)
            l = l + jnp.sum(p_p, axis=-1, keepdims=True)
            acc = acc + jnp.dot(p_p.astype(BF16), vp_ref[:, sl], preferred_element_type=F32)
        o_ref[:, sl] = (acc / l).astype(o_ref.dtype)
        lse_tile = jnp.where(lane == hd, m + jnp.log(l), lse_tile)
    lse_ref[...] = lse_tile


def _attention(qkv, dilation, batch, seq):
    length = seq // dilation
    nb = length // ATT_BLOCK
    has_prev = nb > 1
    blk_shape = (None, None, ATT_BLOCK, ATT_GROUP_WIDTH)

    def cur(part):
        return pl.BlockSpec(blk_shape, lambda b, r, i: (b, r, i, part))

    def prev(part):
        return pl.BlockSpec(blk_shape, lambda b, r, i: (b, r, jnp.maximum(i - 1, 0), part))

    if has_prev:
        in_specs = [cur(0), cur(1), prev(1), cur(2), prev(2)]
    else:
        in_specs = [cur(0), cur(1), cur(2)]
    return pl.pallas_call(
        functools.partial(_attention_kernel, has_prev=has_prev),
        grid=(batch, dilation, nb),
        in_specs=in_specs,
        out_specs=[pl.BlockSpec(blk_shape, lambda b, r, i: (b, r, i, 0)),
                   pl.BlockSpec((None, None, ATT_BLOCK, HEAD_DIM), lambda b, r, i: (b, r, i, 0))],
        out_shape=[jax.ShapeDtypeStruct((batch, dilation, length, ATT_GROUP_WIDTH), BF16),
                   jax.ShapeDtypeStruct((batch, dilation, length, HEAD_DIM), F32)],
        compiler_params=_params(("parallel", "parallel", "arbitrary")),
        name="banded_attention",
    )(*([qkv] * len(in_specs)))


def _merge_out_proj_kernel(o1, o2, o3, l1, l2, l3, h_ref, g_ref, w_ref, out_ref, ot_ref, lt_ref, a_ref):
    tm = h_ref.shape[0]
    for gi, (o_g, l_g) in enumerate(((o2, l2), (o3, l3))):
        dilation = o_g.shape[0]
        n = tm // dilation
        for r in range(dilation):
            rows = pl.ds(r, n, stride=dilation)
            lt_ref[gi, rows, :] = l_g[r]
            for hd in range(ATT_HEADS):
                ot_ref[gi, hd, rows, :] = o_g[r, :, hd * HEAD_DIM:(hd + 1) * HEAD_DIM].astype(F32)
    lses = (l1[0], lt_ref[0], lt_ref[1])
    m = jnp.maximum(jnp.maximum(lses[0], lses[1]), lses[2])
    es = [jnp.exp(v - m) for v in lses]
    den = es[0] + es[1] + es[2]
    ws = [e / den for e in es]
    for hd in range(ATT_HEADS):
        sl = slice(hd * HEAD_DIM, (hd + 1) * HEAD_DIM)
        acc = ws[0][:, hd:hd + 1] * o1[0, :, sl].astype(F32)
        acc = acc + ws[1][:, hd:hd + 1] * ot_ref[0, hd]
        acc = acc + ws[2][:, hd:hd + 1] * ot_ref[1, hd]
        a_ref[:, sl] = acc.astype(BF16)
    y = jnp.dot(a_ref[...], w_ref[...], preferred_element_type=F32)
    out_ref[...] = h_ref[...] + _rms_scale(y, g_ref[...])


def _merge_out_proj(h, gain, w, outs, lses, batch, seq):
    t, d_model = h.shape
    k = w.shape[0]
    tm = TOKEN_TILE
    tiles = seq // tm
    token_tile = lambda b, i: (b * tiles + i, 0)
    fixed = lambda b, i: (0, 0)

    def group_spec(arr):
        dilation, width = arr.shape[1], arr.shape[3]
        return pl.BlockSpec((None, dilation, tm // dilation, width), lambda b, i: (b, 0, i, 0))

    in_specs = [group_spec(a) for a in outs] + [group_spec(a) for a in lses]
    in_specs += [pl.BlockSpec((tm, d_model), token_tile), pl.BlockSpec((1, d_model), fixed),
                 pl.BlockSpec((k, d_model), fixed)]
    return pl.pallas_call(
        _merge_out_proj_kernel,
        grid=(batch, tiles),
        in_specs=in_specs,
        out_specs=pl.BlockSpec((tm, d_model), token_tile),
        out_shape=jax.ShapeDtypeStruct((t, d_model), F32),
        scratch_shapes=[pltpu.VMEM((2, ATT_HEADS, tm, HEAD_DIM), F32), pltpu.VMEM((2, tm, HEAD_DIM), F32),
                        pltpu.VMEM((tm, k), BF16)],
        compiler_params=_params(("parallel", "parallel")),
        name="merge_out_proj",
    )(*outs, *lses, h, gain.reshape(1, d_model), w)


def _out_proj_kernel(a_ref, h_ref, g_ref, w_ref, out_ref):
    y = jnp.dot(a_ref[...], w_ref[...], preferred_element_type=F32)
    out_ref[...] = h_ref[...] + _rms_scale(y, g_ref[...])


def _out_proj(h, gain, w, mixed):
    t, d_model = h.shape
    k = w.shape[0]
    tm = TOKEN_TILE
    row = lambda i: (i, 0)
    fixed = lambda i: (0, 0)
    return pl.pallas_call(
        _out_proj_kernel,
        grid=(t // tm,),
        in_specs=[pl.BlockSpec((tm, k), row), pl.BlockSpec((tm, d_model), row),
                  pl.BlockSpec((1, d_model), fixed), pl.BlockSpec((k, d_model), fixed)],
        out_specs=pl.BlockSpec((tm, d_model), row),
        out_shape=jax.ShapeDtypeStruct((t, d_model), F32),
        compiler_params=_params(("parallel",)),
        name="out_proj",
    )(mixed, h, gain.reshape(1, d_model), w)


def _mlp_kernel(h_ref, g_in_ref, g_out_ref, w1_ref, w2_ref, out_ref, un_ref, acc_ref):
    f = pl.program_id(1)

    @pl.when(f == 0)
    def _():
        un_ref[...] = _rms_scale(h_ref[...], g_in_ref[...]).astype(BF16)
        acc_ref[...] = jnp.zeros_like(acc_ref)

    a = jnp.dot(un_ref[...], w1_ref[...], preferred_element_type=F32)
    a = jnp.square(jnp.maximum(a, 0.0))
    acc_ref[...] += jnp.dot(a.astype(BF16), w2_ref[...], preferred_element_type=F32)

    @pl.when(f == pl.num_programs(1) - 1)
    def _():
        out_ref[...] = h_ref[...] + _rms_scale(acc_ref[...], g_out_ref[...])


def _mlp(h, g_in, g_out, w1, w2):
    t, d_model = h.shape
    d_ff = w1.shape[1]
    tm, tf = 512, 512
    return pl.pallas_call(
        _mlp_kernel,
        grid=(t // tm, d_ff // tf),
        in_specs=[pl.BlockSpec((tm, d_model), lambda m, f: (m, 0)),
                  pl.BlockSpec((1, d_model), lambda m, f: (0, 0)),
                  pl.BlockSpec((1, d_model), lambda m, f: (0, 0)),
                  pl.BlockSpec((d_model, tf), lambda m, f: (0, f)),
                  pl.BlockSpec((tf, d_model), lambda m, f: (f, 0))],
        out_specs=pl.BlockSpec((tm, d_model), lambda m, f: (m, 0)),
        out_shape=jax.ShapeDtypeStruct((t, d_model), F32),
        scratch_shapes=[pltpu.VMEM((tm, d_model), BF16), pltpu.VMEM((tm, d_model), F32)],
        compiler_params=_params(("parallel", "arbitrary")),
        name="mlp",
    )(h, g_in.reshape(1, d_model), g_out.reshape(1, d_model), w1, w2)


def _hgrn_constants():
    r = np.arange(HGRN_CHUNK)[:, None]
    c = np.arange(HGRN_CHUNK)[None, :]
    tri = (c <= r).astype(np.float32)
    msb = np.floor(np.log2(np.maximum(r ^ c, 1))).astype(np.int32)
    lvl = np.where(c < r, msb, -1).astype(np.int32)
    return tri, lvl


def _hgrn_kernel(q_ref, i_ref, gt_ref, f_ref, lbp_ref, gn_ref, tri_ref, lvl_ref, o_ref, st_ref, *, layer_j):
    nh = HGRN_HEADS_PER_STEP
    cs = HGRN_CHUNK
    width = nh * HEAD_DIM

    @pl.when(pl.program_id(2) == 0)
    def _():
        st_ref[...] = jnp.zeros_like(st_ref)

    lbp = lbp_ref[...]
    e = jnp.exp(lbp - jnp.max(lbp, axis=0, keepdims=True))
    p = e / jnp.sum(e, axis=0, keepdims=True)
    csum = p[0:1]
    for t in range(1, layer_j + 1):
        csum = csum + p[t:t + 1]
    lb = csum - p[0:1]
    lb_floor = jnp.maximum(lb, LB_FLOOR)
    one_minus_lb = 1.0 - lb

    z = f_ref[...]
    en = jnp.exp(-jnp.abs(z))
    rcp = 1.0 / (1.0 + en)
    pos = z >= 0.0
    log_f = jnp.log(lb_floor + one_minus_lb * (jnp.where(pos, 1.0, en) * rcp))
    kk = one_minus_lb * (jnp.where(pos, en, 1.0) * rcp) - (lb_floor - lb)

    g_hi = log_f.astype(BF16)
    r1 = log_f - g_hi.astype(F32)
    g_mid = r1.astype(BF16)
    g_lo = (r1 - g_mid.astype(F32)).astype(BF16)
    tri = tri_ref[...]
    b = (jnp.dot(tri, g_hi, preferred_element_type=F32)
         + jnp.dot(tri, g_mid, preferred_element_type=F32)
         + jnp.dot(tri, g_lo, preferred_element_type=F32))
    b_last = b[cs - 1:cs, :]

    row = lax.broadcasted_iota(jnp.int32, (cs, width), 0)
    up = pltpu.roll(log_f, cs - 1, 0)
    down = pltpu.roll(log_f, 1, 0)
    r4 = row & 3
    decays = [
        jnp.where((row & 1) == 1, log_f, 0.0),
        jnp.where(r4 == 0, up, jnp.where(r4 == 1, 0.0, jnp.where(r4 == 2, log_f, log_f + down))),
    ]
    for j in range(2, HGRN_LEVELS):
        half = 1 << j
        nblk = cs // (2 * half)
        b3 = b.reshape(nblk, 2 * half, width)
        mid = jnp.broadcast_to(b3[:, half - 1:half, :], b3.shape).reshape(cs, width)
        decays.append(-jnp.abs(b - mid))
    level_e = [jnp.exp(d) for d in decays]
    e_incl = jnp.exp(b)
    e_suffix = jnp.exp(b_last - b)

    lvl = lvl_ref[...]
    level_mask = [lvl == j for j in range(HGRN_LEVELS)]
    nt = (((1,), (1,)), ((), ()))
    tn = (((0,), (0,)), ((), ()))
    for hh in range(nh):
        sl = slice(hh * HEAD_DIM, (hh + 1) * HEAD_DIM)
        qh = q_ref[:, sl].astype(F32)
        kh = kk[:, sl]
        vb = i_ref[:, sl]
        vh = vb.astype(F32)
        a_mat = jnp.zeros((cs, cs), F32)
        for j in range(HGRN_LEVELS):
            ej = level_e[j][:, sl]
            aj = lax.dot_general((qh * ej).astype(BF16), (kh * ej).astype(BF16), nt,
                                 preferred_element_type=F32)
            a_mat = jnp.where(level_mask[j], aj, a_mat)
        q_dec = (qh * e_incl[:, sl]).astype(BF16)
        k_dec = (kh * e_suffix[:, sl]).astype(BF16)
        st = st_ref[hh]
        inter = lax.dot_general(q_dec, st.astype(BF16), nt, preferred_element_type=F32)
        intra = jnp.dot(a_mat.astype(BF16), vb, preferred_element_type=F32)
        diag = jnp.sum(qh * kh, axis=-1, keepdims=True) * vh
        o = inter + intra + diag
        st_ref[hh] = st * e_incl[cs - 1:cs, sl] + lax.dot_general(vb, k_dec, tn, preferred_element_type=F32)
        gt = gt_ref[:, sl].astype(F32)
        on = _rms_scale(o, gn_ref[...]) * (gt * (1.0 / (1.0 + jnp.exp(-gt))))
        o_ref[:, sl] = on.astype(o_ref.dtype)


def _hgrn_recurrence(qig, fpre, lower_bound_params, layer_j, out_norm_gain, batch, seq, d_model):
    wb = HGRN_HEADS_PER_STEP * HEAD_DIM
    hb = d_model // wb
    tri, lvl = _hgrn_constants()
    n_layers = lower_bound_params.shape[0]

    def part(k):
        return pl.BlockSpec((None, HGRN_CHUNK, wb), lambda b, g, c: (b, c, k * hb + g))

    fixed = lambda b, g, c: (0, 0)
    out = pl.pallas_call(
        functools.partial(_hgrn_kernel, layer_j=layer_j),
        grid=(batch, hb, seq // HGRN_CHUNK),
        in_specs=[part(0), part(1), part(2), part(0),
                  pl.BlockSpec((n_layers, wb), lambda b, g, c: (0, g)),
                  pl.BlockSpec((1, HEAD_DIM), fixed),
                  pl.BlockSpec(tri.shape, fixed),
                  pl.BlockSpec(lvl.shape, fixed)],
        out_specs=pl.BlockSpec((None, HGRN_CHUNK, wb), lambda b, g, c: (b, c, g)),
        out_shape=jax.ShapeDtypeStruct((batch, seq, d_model), BF16),
        scratch_shapes=[pltpu.VMEM((HGRN_HEADS_PER_STEP, HEAD_DIM, HEAD_DIM), F32)],
        compiler_params=_params(("parallel", "parallel", "arbitrary")),
        name="hgrn_recurrence",
    )(qig, qig, qig, fpre, lower_bound_params, out_norm_gain.reshape(1, HEAD_DIM),
      jnp.asarray(tri, BF16), jnp.asarray(lvl))
    return out.reshape(batch * seq, d_model)


@jax.jit
def kernel(x, positions, norm_gains, w_att_in, w_att_out, w_rec_in, rec_lower_bounds, rec_out_norm,
           w_rec_out, w_ff1, w_ff2):
    batch, seq, d_model = x.shape
    depth = norm_gains.shape[0]
    tables = [_rope_tables(positions, dilation) for _, dilation in DILATED_GROUPS]
    h = x.reshape(batch * seq, d_model)
    for layer in range(depth):
        g = norm_gains[layer]
        j = layer // 2
        if layer % 2 == 0:
            w_in = w_att_in[j].astype(BF16)
            group_w = 3 * ATT_GROUP_WIDTH
            outs, lses = [], []
            for gi, (window, dilation) in enumerate(DILATED_GROUPS):
                assert window // dilation == ATT_BLOCK
                qkv = _in_proj(h, g[0], w_in, lambda n, gi=gi: gi + n, group_w, TOKEN_TILE, group_w,
                               dilation, batch, seq, BF16, tables[gi])
                o, lse = _attention(qkv, dilation, batch, seq)
                outs.append(o)
                lses.append(lse)
            h = _merge_out_proj(h, g[1], w_att_out[j].astype(BF16), outs, lses, batch, seq)
        else:
            w_in = w_rec_in[j].astype(BF16)
            tn = 1024
            per_part = d_model // tn
            qig = _in_proj(h, g[0], w_in, lambda n: n + per_part * (n >= per_part), 3 * d_model, 512, tn,
                           1, batch, seq, BF16)
            fpre = _in_proj(h, g[0], w_in, lambda n: n + per_part, d_model, 512, tn, 1, batch, seq, F32)
            mixed = _hgrn_recurrence(qig.reshape(batch, seq, 3 * d_model), fpre.reshape(batch, seq, d_model),
                                     rec_lower_bounds, j, rec_out_norm[j], batch, seq, d_model)
            h = _out_proj(h, g[1], w_rec_out[j].astype(BF16), mixed)
        h = _mlp(h, g[2], g[3], w_ff1[layer].astype(BF16), w_ff2[layer].astype(BF16))
    return h.reshape(batch, seq, d_model)
```

```python
import functools

import numpy as np
import jax
import jax.numpy as jnp
from jax import lax
from jax.experimental import pallas as pl
from jax.experimental.pallas import tpu as pltpu

F32 = jnp.float32
BF16 = jnp.bfloat16

NORM_EPS = 1e-6
MASK_VALUE = -1e30
LB_FLOOR = 1e-30

DILATED_GROUPS = ((128, 1), (512, 4), (2048, 16))
HEAD_DIM = 128
ATT_HEADS = 8
ATT_BLOCK = 128
ATT_GROUP_WIDTH = ATT_HEADS * HEAD_DIM
ROPE_THETA = 500000.0
ROPE_DIM = HEAD_DIM // 4
ROPE_HALF = ROPE_DIM // 2
TOKEN_TILE = 256
PROJ_TILE_M, PROJ_TILE_N = 1024, 512
MLP_TILE_M, MLP_TILE_F = 1024, 512
HGRN_CHUNK = 128
HGRN_LEVELS = 7
HGRN_HEADS_PER_STEP = 4

VMEM_LIMIT_BYTES = 56 * 1024 * 1024
SINGLE = pl.Buffered(1)


def _params(semantics):
    return pltpu.CompilerParams(dimension_semantics=semantics, vmem_limit_bytes=VMEM_LIMIT_BYTES)


def _rms_scale(x, gain):
    ms = jnp.mean(x * x, axis=-1, keepdims=True)
    return x * lax.rsqrt(ms + NORM_EPS) * gain


def _rope_kernel(pos_ref, invf_ref, cos_ref, sa_ref, sb_ref):
    ang = pos_ref[...].astype(F32) * invf_ref[...]
    lane = lax.broadcasted_iota(jnp.int32, ang.shape, 1)
    c = jnp.cos(ang)
    s = jnp.sin(ang)
    cos_ref[...] = jnp.where(lane < ROPE_DIM, c, 1.0)
    sa_ref[...] = jnp.where((lane >= ROPE_HALF) & (lane < ROPE_DIM), s, 0.0)
    sb_ref[...] = jnp.where(lane < ROPE_HALF, -s, 0.0)


def _rope_tables(positions, dilation):
    batch, seq = positions.shape
    t = batch * seq
    tr = 1024
    pos = positions.reshape(batch, seq // dilation, dilation).transpose(0, 2, 1)
    inv_freq = ROPE_THETA ** (-jnp.arange(ROPE_HALF, dtype=F32) / ROPE_HALF)
    invf = jnp.zeros((1, HEAD_DIM), F32).at[0, :ROPE_DIM].set(jnp.tile(inv_freq, 2))
    out = jax.ShapeDtypeStruct((t, HEAD_DIM), F32)
    tabs = pl.pallas_call(
        _rope_kernel,
        grid=(t // tr,),
        in_specs=[pl.BlockSpec((tr, 1), lambda i: (i, 0)),
                  pl.BlockSpec((1, HEAD_DIM), lambda i: (0, 0))],
        out_specs=[pl.BlockSpec((tr, HEAD_DIM), lambda i: (i, 0))] * 3,
        out_shape=[out, out, out],
        compiler_params=_params(("parallel",)),
        name="rope_tables",
    )(pos.reshape(t, 1), invf)
    return [tab.reshape(batch, dilation, seq // dilation, HEAD_DIM) for tab in tabs]


def _att_in_proj_kernel(*refs, dilation):
    refs = list(refs)
    planes_ref = refs.pop() if dilation > 1 else None
    x_ref, g_ref, w_ref, cos_ref, sa_ref, sb_ref, o_ref, wb_ref, xn_ref = refs
    tm, d_model = x_ref.shape
    n = tm // dilation
    part = pl.program_id(0)

    @pl.when((pl.program_id(1) == 0) & (pl.program_id(2) == 0))
    def _():
        wb_ref[...] = w_ref[...].astype(BF16)

    xn = _rms_scale(x_ref[...], g_ref[...])
    if dilation == 1:
        xn_ref[...] = xn.astype(BF16)
    else:
        for c in range(d_model // HEAD_DIM):
            planes_ref[c] = xn[:, c * HEAD_DIM:(c + 1) * HEAD_DIM]
        for r in range(dilation):
            for c in range(d_model // HEAD_DIM):
                rows = planes_ref[c, pl.ds(r, n, stride=dilation), :]
                xn_ref[r * n:(r + 1) * n, c * HEAD_DIM:(c + 1) * HEAD_DIM] = rows.astype(BF16)

    acc = jnp.dot(xn_ref[...], wb_ref[...], preferred_element_type=F32)

    def store(lo, val):
        for r in range(dilation):
            o_ref[r, :, lo:lo + val.shape[1]] = val[r * n:(r + 1) * n].astype(o_ref.dtype)

    @pl.when(part < 2)
    def _():
        mult = jnp.where(part == 0, HEAD_DIM ** -0.5, 1.0)
        c = cos_ref[...].reshape(tm, HEAD_DIM) * mult
        a = sa_ref[...].reshape(tm, HEAD_DIM) * mult
        b = sb_ref[...].reshape(tm, HEAD_DIM) * mult
        for hd in range(ATT_HEADS):
            lo = hd * HEAD_DIM
            xh = acc[:, lo:lo + HEAD_DIM]
            rot = (xh * c + pltpu.roll(xh, ROPE_HALF, 1) * a
                   + pltpu.roll(xh, HEAD_DIM - ROPE_HALF, 1) * b)
            store(lo, rot)

    @pl.when(part == 2)
    def _():
        store(0, acc)


def _att_in_proj(h, gain, w, layer, group, dilation, batch, seq, tables):
    t, d_model = h.shape
    tm = TOKEN_TILE
    tiles = seq // tm
    n = tm // dilation
    tn = ATT_GROUP_WIDTH
    residue_tile = lambda p, b, i: (b, 0, i, 0)
    fixed = lambda p, b, i: (0, 0)
    scratch = [pltpu.VMEM((d_model, tn), BF16), pltpu.VMEM((tm, d_model), BF16)]
    if dilation > 1:
        scratch.append(pltpu.VMEM((d_model // HEAD_DIM, tm, HEAD_DIM), F32))
    return pl.pallas_call(
        functools.partial(_att_in_proj_kernel, dilation=dilation),
        grid=(3, batch, tiles),
        in_specs=[pl.BlockSpec((tm, d_model), lambda p, b, i: (b * tiles + i, 0)),
                  pl.BlockSpec((1, d_model), fixed),
                  pl.BlockSpec((None, d_model, tn), lambda p, b, i: (layer, 0, 3 * group + p))]
                 + [pl.BlockSpec((None, dilation, n, HEAD_DIM), residue_tile)] * 3,
        out_specs=pl.BlockSpec((None, dilation, n, tn), lambda p, b, i: (b, 0, i, p)),
        out_shape=jax.ShapeDtypeStruct((batch, dilation, seq // dilation, 3 * tn), BF16),
        scratch_shapes=scratch,
        compiler_params=_params(("arbitrary", "arbitrary", "arbitrary")),
        name="att_in_proj",
    )(h, gain.reshape(1, d_model), w, *tables)


def _in_proj_kernel(x_ref, g_ref, w_ref, o_ref, xn_ref):
    @pl.when(pl.program_id(1) == 0)
    def _():
        xn_ref[...] = _rms_scale(x_ref[...], g_ref[...]).astype(BF16)

    acc = jnp.dot(xn_ref[...], w_ref[...].astype(BF16), preferred_element_type=F32)
    o_ref[...] = acc.astype(o_ref.dtype)


def _in_proj(h, gain, w, layer, col_block, width, out_dtype):
    t, d_model = h.shape
    tm, tn = PROJ_TILE_M, PROJ_TILE_N
    return pl.pallas_call(
        _in_proj_kernel,
        grid=(t // tm, width // tn),
        in_specs=[pl.BlockSpec((tm, d_model), lambda i, j: (i, 0)),
                  pl.BlockSpec((1, d_model), lambda i, j: (0, 0)),
                  pl.BlockSpec((None, d_model, tn), lambda i, j: (layer, 0, col_block(j)))],
        out_specs=pl.BlockSpec((tm, tn), lambda i, j: (i, j)),
        out_shape=jax.ShapeDtypeStruct((t, width), out_dtype),
        scratch_shapes=[pltpu.VMEM((tm, d_model), BF16)],
        compiler_params=_params(("parallel", "arbitrary")),
        name="in_proj",
    )(h, gain.reshape(1, d_model), w)


def _attention_kernel(*refs, has_prev):
    if has_prev:
        q_ref, kc_ref, kp_ref, vc_ref, vp_ref, o_ref, lse_ref, s_ref = refs
    else:
        q_ref, kc_ref, vc_ref, o_ref, lse_ref, s_ref = refs
    n_keys = s_ref.shape[2]
    row = lax.broadcasted_iota(jnp.int32, (ATT_BLOCK, n_keys), 0)
    col = lax.broadcasted_iota(jnp.int32, (ATT_BLOCK, n_keys), 1)
    if has_prev:
        first_block = pl.program_id(2) == 0
        mask = (col >= row) & (col <= row + ATT_BLOCK) & ((col >= ATT_BLOCK) | jnp.logical_not(first_block))
    else:
        mask = col <= row
    nt = (((1,), (1,)), ((), ()))
    heads = [slice(hd * HEAD_DIM, (hd + 1) * HEAD_DIM) for hd in range(ATT_HEADS)]

    def both(prev_ref, cur_ref, sl):
        if has_prev:
            return jnp.concatenate([prev_ref[:, sl], cur_ref[:, sl]], axis=0)
        return cur_ref[:, sl]

    maxes = []
    for hd, sl in enumerate(heads):
        s = lax.dot_general(q_ref[:, sl], both(kp_ref if has_prev else None, kc_ref, sl), nt,
                            preferred_element_type=F32)
        s = jnp.where(mask, s, MASK_VALUE)
        s_ref[hd] = s
        maxes.append(jnp.max(s, axis=-1, keepdims=True))
    lane = lax.broadcasted_iota(jnp.int32, (ATT_BLOCK, HEAD_DIM), 1)
    lse_tile = jnp.zeros((ATT_BLOCK, HEAD_DIM), F32)
    for hd, sl in enumerate(heads):
        p = jnp.exp(s_ref[hd] - maxes[hd])
        l = jnp.sum(p, axis=-1, keepdims=True)
        acc = jnp.dot(p.astype(BF16), both(vp_ref if has_prev else None, vc_ref, sl),
                      preferred_element_type=F32)
        o_ref[:, sl] = (acc / l).astype(o_ref.dtype)
        lse_tile = jnp.where(lane == hd, maxes[hd] + jnp.log(l), lse_tile)
    lse_ref[...] = lse_tile


def _attention(qkv, dilation, batch, seq):
    length = seq // dilation
    nb = length // ATT_BLOCK
    has_prev = nb > 1
    blk_shape = (None, None, ATT_BLOCK, ATT_GROUP_WIDTH)

    def cur(part):
        return pl.BlockSpec(blk_shape, lambda b, r, i: (b, r, i, part))

    def prev(part):
        return pl.BlockSpec(blk_shape, lambda b, r, i: (b, r, jnp.maximum(i - 1, 0), part))

    if has_prev:
        in_specs = [cur(0), cur(1), prev(1), cur(2), prev(2)]
    else:
        in_specs = [cur(0), cur(1), cur(2)]
    n_keys = 2 * ATT_BLOCK if has_prev else ATT_BLOCK
    return pl.pallas_call(
        functools.partial(_attention_kernel, has_prev=has_prev),
        grid=(batch, dilation, nb),
        in_specs=in_specs,
        out_specs=[pl.BlockSpec(blk_shape, lambda b, r, i: (b, r, i, 0)),
                   pl.BlockSpec((None, None, ATT_BLOCK, HEAD_DIM), lambda b, r, i: (b, r, i, 0))],
        out_shape=[jax.ShapeDtypeStruct((batch, dilation, length, ATT_GROUP_WIDTH), BF16),
                   jax.ShapeDtypeStruct((batch, dilation, length, HEAD_DIM), F32)],
        scratch_shapes=[pltpu.VMEM((ATT_HEADS, ATT_BLOCK, n_keys), F32)],
        compiler_params=_params(("parallel", "parallel", "arbitrary")),
        name="banded_attention",
    )(*([qkv] * len(in_specs)))


def _cast_weight_once(w_ref, wb_ref, first):
    @pl.when(first)
    def _():
        wb_ref[...] = w_ref[...].astype(BF16)


def _merge_out_proj_kernel(o1, o2, o3, l1, l2, l3, h_ref, g_ref, w_ref, out_ref, wb_ref, ot_ref, lt_ref, a_ref):
    _cast_weight_once(w_ref, wb_ref, (pl.program_id(0) == 0) & (pl.program_id(1) == 0))
    tm = h_ref.shape[0]
    for gi, (o_g, l_g) in enumerate(((o2, l2), (o3, l3))):
        dilation = o_g.shape[0]
        n = tm // dilation
        for r in range(dilation):
            rows = pl.ds(r, n, stride=dilation)
            lt_ref[gi, rows, :] = l_g[r]
            for hd in range(ATT_HEADS):
                ot_ref[gi, hd, rows, :] = o_g[r, :, hd * HEAD_DIM:(hd + 1) * HEAD_DIM].astype(F32)
    lses = (l1[0], lt_ref[0], lt_ref[1])
    m = jnp.maximum(jnp.maximum(lses[0], lses[1]), lses[2])
    es = [jnp.exp(v - m) for v in lses]
    den = es[0] + es[1] + es[2]
    ws = [e / den for e in es]
    for hd in range(ATT_HEADS):
        sl = slice(hd * HEAD_DIM, (hd + 1) * HEAD_DIM)
        acc = ws[0][:, hd:hd + 1] * o1[0, :, sl].astype(F32)
        acc = acc + ws[1][:, hd:hd + 1] * ot_ref[0, hd]
        acc = acc + ws[2][:, hd:hd + 1] * ot_ref[1, hd]
        a_ref[:, sl] = acc.astype(BF16)
    y = jnp.dot(a_ref[...], wb_ref[...], preferred_element_type=F32)
    out_ref[...] = h_ref[...] + _rms_scale(y, g_ref[...])


def _merge_out_proj(h, gain, w, layer, outs, lses, batch, seq):
    t, d_model = h.shape
    k = w.shape[1]
    tm = TOKEN_TILE
    tiles = seq // tm
    token_tile = lambda b, i: (b * tiles + i, 0)
    fixed = lambda b, i: (0, 0)

    def group_spec(arr):
        dilation, width = arr.shape[1], arr.shape[3]
        return pl.BlockSpec((None, dilation, tm // dilation, width), lambda b, i: (b, 0, i, 0))

    in_specs = [group_spec(a) for a in outs] + [group_spec(a) for a in lses]
    in_specs += [pl.BlockSpec((tm, d_model), token_tile), pl.BlockSpec((1, d_model), fixed),
                 pl.BlockSpec((None, k, d_model), lambda b, i: (layer, 0, 0), pipeline_mode=SINGLE)]
    return pl.pallas_call(
        _merge_out_proj_kernel,
        grid=(batch, tiles),
        in_specs=in_specs,
        out_specs=pl.BlockSpec((tm, d_model), token_tile),
        out_shape=jax.ShapeDtypeStruct((t, d_model), F32),
        scratch_shapes=[pltpu.VMEM((k, d_model), BF16),
                        pltpu.VMEM((2, ATT_HEADS, tm, HEAD_DIM), F32), pltpu.VMEM((2, tm, HEAD_DIM), F32),
                        pltpu.VMEM((tm, k), BF16)],
        compiler_params=_params(("arbitrary", "arbitrary")),
        name="merge_out_proj",
    )(*outs, *lses, h, gain.reshape(1, d_model), w)


def _out_proj_kernel(a_ref, h_ref, g_ref, w_ref, out_ref, wb_ref):
    _cast_weight_once(w_ref, wb_ref, pl.program_id(0) == 0)
    y = jnp.dot(a_ref[...], wb_ref[...], preferred_element_type=F32)
    out_ref[...] = h_ref[...] + _rms_scale(y, g_ref[...])


def _out_proj(h, gain, w, layer, mixed):
    t, d_model = h.shape
    k = w.shape[1]
    tm = TOKEN_TILE
    row = lambda i: (i, 0)
    fixed = lambda i: (0, 0)
    return pl.pallas_call(
        _out_proj_kernel,
        grid=(t // tm,),
        in_specs=[pl.BlockSpec((tm, k), row), pl.BlockSpec((tm, d_model), row),
                  pl.BlockSpec((1, d_model), fixed),
                  pl.BlockSpec((None, k, d_model), lambda i: (layer, 0, 0), pipeline_mode=SINGLE)],
        out_specs=pl.BlockSpec((tm, d_model), row),
        out_shape=jax.ShapeDtypeStruct((t, d_model), F32),
        scratch_shapes=[pltpu.VMEM((k, d_model), BF16)],
        compiler_params=_params(("arbitrary",)),
        name="out_proj",
    )(mixed, h, gain.reshape(1, d_model), w)


def _mlp_kernel(h_ref, g_in_ref, g_out_ref, w1_ref, w2_ref, out_ref, un_ref, acc_ref):
    f = pl.program_id(1)

    @pl.when(f == 0)
    def _():
        un_ref[...] = _rms_scale(h_ref[...], g_in_ref[...]).astype(BF16)
        acc_ref[...] = jnp.zeros_like(acc_ref)

    a = jnp.dot(un_ref[...], w1_ref[...].astype(BF16), preferred_element_type=F32)
    a = jnp.square(jnp.maximum(a, 0.0))
    acc_ref[...] += jnp.dot(a.astype(BF16), w2_ref[...].astype(BF16), preferred_element_type=F32)

    @pl.when(f == pl.num_programs(1) - 1)
    def _():
        out_ref[...] = h_ref[...] + _rms_scale(acc_ref[...], g_out_ref[...])


def _mlp(h, g_in, g_out, w1, w2, layer):
    t, d_model = h.shape
    d_ff = w1.shape[2]
    tm, tf = MLP_TILE_M, MLP_TILE_F
    return pl.pallas_call(
        _mlp_kernel,
        grid=(t // tm, d_ff // tf),
        in_specs=[pl.BlockSpec((tm, d_model), lambda m, f: (m, 0), pipeline_mode=SINGLE),
                  pl.BlockSpec((1, d_model), lambda m, f: (0, 0)),
                  pl.BlockSpec((1, d_model), lambda m, f: (0, 0)),
                  pl.BlockSpec((None, d_model, tf), lambda m, f: (layer, 0, f)),
                  pl.BlockSpec((None, tf, d_model), lambda m, f: (layer, f, 0))],
        out_specs=pl.BlockSpec((tm, d_model), lambda m, f: (m, 0), pipeline_mode=SINGLE),
        out_shape=jax.ShapeDtypeStruct((t, d_model), F32),
        scratch_shapes=[pltpu.VMEM((tm, d_model), BF16), pltpu.VMEM((tm, d_model), F32)],
        compiler_params=_params(("parallel", "arbitrary")),
        name="mlp",
    )(h, g_in.reshape(1, d_model), g_out.reshape(1, d_model), w1, w2)


def _hgrn_constants():
    r = np.arange(HGRN_CHUNK)[:, None]
    c = np.arange(HGRN_CHUNK)[None, :]
    tri = (c <= r).astype(np.float32)
    msb = np.floor(np.log2(np.maximum(r ^ c, 1))).astype(np.int32)
    lvl = np.where(c < r, msb, -1).astype(np.int32)
    return tri, lvl


def _hgrn_kernel(q_ref, i_ref, gt_ref, f_ref, lbp_ref, gn_ref, tri_ref, lvl_ref, o_ref, st_ref, *, layer_j):
    nh = HGRN_HEADS_PER_STEP
    cs = HGRN_CHUNK
    width = nh * HEAD_DIM

    @pl.when(pl.program_id(2) == 0)
    def _():
        st_ref[...] = jnp.zeros_like(st_ref)

    lbp = lbp_ref[...]
    e = jnp.exp(lbp - jnp.max(lbp, axis=0, keepdims=True))
    p = e / jnp.sum(e, axis=0, keepdims=True)
    csum = p[0:1]
    for t in range(1, layer_j + 1):
        csum = csum + p[t:t + 1]
    lb = csum - p[0:1]
    lb_floor = jnp.maximum(lb, LB_FLOOR)
    one_minus_lb = 1.0 - lb

    z = f_ref[...]
    en = jnp.exp(-jnp.abs(z))
    rcp = 1.0 / (1.0 + en)
    pos = z >= 0.0
    log_f = jnp.log(lb_floor + one_minus_lb * (jnp.where(pos, 1.0, en) * rcp))
    kk = one_minus_lb * (jnp.where(pos, en, 1.0) * rcp) - (lb_floor - lb)

    g_hi = log_f.astype(BF16)
    r1 = log_f - g_hi.astype(F32)
    g_mid = r1.astype(BF16)
    g_lo = (r1 - g_mid.astype(F32)).astype(BF16)
    tri = tri_ref[...]
    b = (jnp.dot(tri, g_hi, preferred_element_type=F32)
         + jnp.dot(tri, g_mid, preferred_element_type=F32)
         + jnp.dot(tri, g_lo, preferred_element_type=F32))
    b_last = b[cs - 1:cs, :]

    row = lax.broadcasted_iota(jnp.int32, (cs, width), 0)
    up = pltpu.roll(log_f, cs - 1, 0)
    down = pltpu.roll(log_f, 1, 0)
    r4 = row & 3
    decays = [
        jnp.where((row & 1) == 1, log_f, 0.0),
        jnp.where(r4 == 0, up, jnp.where(r4 == 1, 0.0, jnp.where(r4 == 2, log_f, log_f + down))),
    ]
    for j in range(2, HGRN_LEVELS):
        half = 1 << j
        nblk = cs // (2 * half)
        b3 = b.reshape(nblk, 2 * half, width)
        mid = jnp.broadcast_to(b3[:, half - 1:half, :], b3.shape).reshape(cs, width)
        decays.append(-jnp.abs(b - mid))
    level_e = [jnp.exp(d) for d in decays]
    e_incl = jnp.exp(b)
    e_suffix = jnp.exp(b_last - b)

    lvl = lvl_ref[...]
    level_mask = [lvl == j for j in range(HGRN_LEVELS)]
    nt = (((1,), (1,)), ((), ()))
    tn = (((0,), (0,)), ((), ()))
    for hh in range(nh):
        sl = slice(hh * HEAD_DIM, (hh + 1) * HEAD_DIM)
        qh = q_ref[:, sl].astype(F32)
        kh = kk[:, sl]
        vb = i_ref[:, sl]
        vh = vb.astype(F32)
        a_mat = jnp.zeros((cs, cs), F32)
        for j in range(HGRN_LEVELS):
            ej = level_e[j][:, sl]
            aj = lax.dot_general((qh * ej).astype(BF16), (kh * ej).astype(BF16), nt,
                                 preferred_element_type=F32)
            a_mat = jnp.where(level_mask[j], aj, a_mat)
        q_dec = (qh * e_incl[:, sl]).astype(BF16)
        k_dec = (kh * e_suffix[:, sl]).astype(BF16)
        st = st_ref[hh]
        inter = lax.dot_general(q_dec, st.astype(BF16), nt, preferred_element_type=F32)
        intra = jnp.dot(a_mat.astype(BF16), vb, preferred_element_type=F32)
        diag = jnp.sum(qh * kh, axis=-1, keepdims=True) * vh
        o = inter + intra + diag
        st_ref[hh] = st * e_incl[cs - 1:cs, sl] + lax.dot_general(vb, k_dec, tn, preferred_element_type=F32)
        gt = gt_ref[:, sl].astype(F32)
        on = _rms_scale(o, gn_ref[...]) * (gt * (1.0 / (1.0 + jnp.exp(-gt))))
        o_ref[:, sl] = on.astype(o_ref.dtype)


def _hgrn_recurrence(qig, fpre, lower_bound_params, layer_j, out_norm_gain, batch, seq, d_model):
    wb = HGRN_HEADS_PER_STEP * HEAD_DIM
    hb = d_model // wb
    tri, lvl = _hgrn_constants()
    n_layers = lower_bound_params.shape[0]

    def part(k):
        return pl.BlockSpec((None, HGRN_CHUNK, wb), lambda b, g, c: (b, c, k * hb + g))

    fixed = lambda b, g, c: (0, 0)
    out = pl.pallas_call(
        functools.partial(_hgrn_kernel, layer_j=layer_j),
        grid=(batch, hb, seq // HGRN_CHUNK),
        in_specs=[part(0), part(1), part(2), part(0),
                  pl.BlockSpec((n_layers, wb), lambda b, g, c: (0, g)),
                  pl.BlockSpec((1, HEAD_DIM), fixed),
                  pl.BlockSpec(tri.shape, fixed),
                  pl.BlockSpec(lvl.shape, fixed)],
        out_specs=pl.BlockSpec((None, HGRN_CHUNK, wb), lambda b, g, c: (b, c, g)),
        out_shape=jax.ShapeDtypeStruct((batch, seq, d_model), BF16),
        scratch_shapes=[pltpu.VMEM((HGRN_HEADS_PER_STEP, HEAD_DIM, HEAD_DIM), F32)],
        compiler_params=_params(("parallel", "parallel", "arbitrary")),
        name="hgrn_recurrence",
    )(qig, qig, qig, fpre, lower_bound_params, out_norm_gain.reshape(1, HEAD_DIM),
      jnp.asarray(tri, BF16), jnp.asarray(lvl))
    return out.reshape(batch * seq, d_model)


@jax.jit
def kernel(x, positions, norm_gains, w_att_in, w_att_out, w_rec_in, rec_lower_bounds, rec_out_norm,
           w_rec_out, w_ff1, w_ff2):
    batch, seq, d_model = x.shape
    depth = norm_gains.shape[0]
    tables = [_rope_tables(positions, dilation) for _, dilation in DILATED_GROUPS]
    h = x.reshape(batch * seq, d_model)
    for layer in range(depth):
        g = norm_gains[layer]
        j = layer // 2
        if layer % 2 == 0:
            outs, lses = [], []
            for gi, (window, dilation) in enumerate(DILATED_GROUPS):
                assert window // dilation == ATT_BLOCK
                qkv = _att_in_proj(h, g[0], w_att_in, j, gi, dilation, batch, seq, tables[gi])
                o, lse = _attention(qkv, dilation, batch, seq)
                outs.append(o)
                lses.append(lse)
            h = _merge_out_proj(h, g[1], w_att_out, j, outs, lses, batch, seq)
        else:
            per_part = d_model // PROJ_TILE_N
            qig = _in_proj(h, g[0], w_rec_in, j, lambda n: n + per_part * (n >= per_part), 3 * d_model, BF16)
            fpre = _in_proj(h, g[0], w_rec_in, j, lambda n: n + per_part, d_model, F32)
            mixed = _hgrn_recurrence(qig.reshape(batch, seq, 3 * d_model), fpre.reshape(batch, seq, d_model),
                                     rec_lower_bounds, j, rec_out_norm[j], batch, seq, d_model)
            h = _out_proj(h, g[1], w_rec_out, j, mixed)
        h = _mlp(h, g[2], g[3], w_ff1, w_ff2, layer)
    return h.reshape(batch, seq, d_model)
```

```python
import functools

import numpy as np
import jax
import jax.numpy as jnp
from jax import lax
from jax.experimental import pallas as pl
from jax.experimental.pallas import tpu as pltpu

F32 = jnp.float32
BF16 = jnp.bfloat16

NORM_EPS = 1e-6
MASK_VALUE = -1e30
LB_FLOOR = 1e-30

DILATED_GROUPS = ((128, 1), (512, 4), (2048, 16))
HEAD_DIM = 128
ATT_HEADS = 8
ATT_BLOCK = 128
ATT_GROUP_WIDTH = ATT_HEADS * HEAD_DIM
ROPE_THETA = 500000.0
ROPE_DIM = HEAD_DIM // 4
ROPE_HALF = ROPE_DIM // 2
TOKEN_TILE = 256
PROJ_TILE_M, PROJ_TILE_N = 1024, 512
MLP_TILE_M, MLP_TILE_F = 1024, 512
HGRN_CHUNK = 128
HGRN_LEVELS = 7
HGRN_HEADS_PER_STEP = 8
LOG2E = 1.4426950408889634

VMEM_LIMIT_BYTES = 56 * 1024 * 1024
SINGLE = pl.Buffered(1)


def _params(semantics):
    return pltpu.CompilerParams(dimension_semantics=semantics, vmem_limit_bytes=VMEM_LIMIT_BYTES)


def _rms_scale(x, gain):
    ms = jnp.mean(x * x, axis=-1, keepdims=True)
    return x * lax.rsqrt(ms + NORM_EPS) * gain


def _rope_kernel(pos_ref, invf_ref, cos_ref, sa_ref, sb_ref):
    ang = pos_ref[...].astype(F32) * invf_ref[...]
    lane = lax.broadcasted_iota(jnp.int32, ang.shape, 1)
    c = jnp.cos(ang)
    s = jnp.sin(ang)
    cos_ref[...] = jnp.where(lane < ROPE_DIM, c, 1.0)
    sa_ref[...] = jnp.where((lane >= ROPE_HALF) & (lane < ROPE_DIM), s, 0.0)
    sb_ref[...] = jnp.where(lane < ROPE_HALF, -s, 0.0)


def _rope_tables(positions, dilation):
    batch, seq = positions.shape
    t = batch * seq
    tr = 1024
    pos = positions.reshape(batch, seq // dilation, dilation).transpose(0, 2, 1)
    inv_freq = ROPE_THETA ** (-jnp.arange(ROPE_HALF, dtype=F32) / ROPE_HALF)
    invf = jnp.zeros((1, HEAD_DIM), F32).at[0, :ROPE_DIM].set(jnp.tile(inv_freq, 2))
    out = jax.ShapeDtypeStruct((t, HEAD_DIM), F32)
    tabs = pl.pallas_call(
        _rope_kernel,
        grid=(t // tr,),
        in_specs=[pl.BlockSpec((tr, 1), lambda i: (i, 0)),
                  pl.BlockSpec((1, HEAD_DIM), lambda i: (0, 0))],
        out_specs=[pl.BlockSpec((tr, HEAD_DIM), lambda i: (i, 0))] * 3,
        out_shape=[out, out, out],
        compiler_params=_params(("parallel",)),
        name="rope_tables",
    )(pos.reshape(t, 1), invf)
    return [tab.reshape(batch, dilation, seq // dilation, HEAD_DIM) for tab in tabs]


def _att_in_proj_kernel(*refs, dilation):
    refs = list(refs)
    planes_ref = refs.pop() if dilation > 1 else None
    x_ref, g_ref, w_ref, cos_ref, sa_ref, sb_ref, o_ref, wb_ref, xn_ref = refs
    tm, d_model = x_ref.shape
    n = tm // dilation

    @pl.when((pl.program_id(0) == 0) & (pl.program_id(1) == 0))
    def _():
        def cast_rows(i, carry):
            rows = pl.ds(pl.multiple_of(i * HEAD_DIM, HEAD_DIM), HEAD_DIM)
            wb_ref[rows, :] = w_ref[rows, :].astype(BF16)
            return carry
        lax.fori_loop(0, d_model // HEAD_DIM, cast_rows, 0)

    xn = _rms_scale(x_ref[...], g_ref[...])
    if dilation == 1:
        xn_ref[...] = xn.astype(BF16)
    else:
        for c in range(d_model // HEAD_DIM):
            planes_ref[c] = xn[:, c * HEAD_DIM:(c + 1) * HEAD_DIM]
        for r in range(dilation):
            for c in range(d_model // HEAD_DIM):
                rows = planes_ref[c, pl.ds(r, n, stride=dilation), :]
                xn_ref[r * n:(r + 1) * n, c * HEAD_DIM:(c + 1) * HEAD_DIM] = rows.astype(BF16)

    acc = jnp.dot(xn_ref[...], wb_ref[...], preferred_element_type=F32)

    def store(lo, val):
        for r in range(dilation):
            o_ref[r, :, lo:lo + val.shape[1]] = val[r * n:(r + 1) * n].astype(o_ref.dtype)

    cos, sa, sb = (ref[...].reshape(tm, HEAD_DIM) for ref in (cos_ref, sa_ref, sb_ref))
    scale = HEAD_DIM ** -0.5
    for part, mult in ((0, scale), (1, 1.0)):
        c, a, b = cos * mult, sa * mult, sb * mult
        for hd in range(ATT_HEADS):
            lo = part * ATT_GROUP_WIDTH + hd * HEAD_DIM
            xh = acc[:, lo:lo + HEAD_DIM]
            rot = (xh * c + pltpu.roll(xh, ROPE_HALF, 1) * a
                   + pltpu.roll(xh, HEAD_DIM - ROPE_HALF, 1) * b)
            store(lo, rot)
    store(2 * ATT_GROUP_WIDTH, acc[:, 2 * ATT_GROUP_WIDTH:])


def _att_in_proj(h, gain, w, layer, group, dilation, batch, seq, tables):
    t, d_model = h.shape
    tm = TOKEN_TILE
    tiles = seq // tm
    n = tm // dilation
    width = 3 * ATT_GROUP_WIDTH
    residue_tile = lambda b, i: (b, 0, i, 0)
    scratch = [pltpu.VMEM((d_model, width), BF16), pltpu.VMEM((tm, d_model), BF16)]
    if dilation > 1:
        scratch.append(pltpu.VMEM((d_model // HEAD_DIM, tm, HEAD_DIM), F32))
    return pl.pallas_call(
        functools.partial(_att_in_proj_kernel, dilation=dilation),
        grid=(batch, tiles),
        in_specs=[pl.BlockSpec((tm, d_model), lambda b, i: (b * tiles + i, 0)),
                  pl.BlockSpec((1, d_model), lambda b, i: (0, 0)),
                  pl.BlockSpec((None, d_model, width), lambda b, i: (layer, 0, group), pipeline_mode=SINGLE)]
                 + [pl.BlockSpec((None, dilation, n, HEAD_DIM), residue_tile)] * 3,
        out_specs=pl.BlockSpec((None, dilation, n, width), residue_tile),
        out_shape=jax.ShapeDtypeStruct((batch, dilation, seq // dilation, width), BF16),
        scratch_shapes=scratch,
        compiler_params=_params(("arbitrary", "arbitrary")),
        name="att_in_proj",
    )(h, gain.reshape(1, d_model), w, *tables)


def _in_proj_kernel(x_ref, g_ref, w_ref, o_ref, xn_ref):
    @pl.when(pl.program_id(1) == 0)
    def _():
        xn_ref[...] = _rms_scale(x_ref[...], g_ref[...]).astype(BF16)

    acc = jnp.dot(xn_ref[...], w_ref[...].astype(BF16), preferred_element_type=F32)
    o_ref[...] = acc.astype(o_ref.dtype)


def _in_proj(h, gain, w, layer, col_block, width, out_dtype):
    t, d_model = h.shape
    tm, tn = PROJ_TILE_M, PROJ_TILE_N
    return pl.pallas_call(
        _in_proj_kernel,
        grid=(t // tm, width // tn),
        in_specs=[pl.BlockSpec((tm, d_model), lambda i, j: (i, 0)),
                  pl.BlockSpec((1, d_model), lambda i, j: (0, 0)),
                  pl.BlockSpec((None, d_model, tn), lambda i, j: (layer, 0, col_block(j)))],
        out_specs=pl.BlockSpec((tm, tn), lambda i, j: (i, j)),
        out_shape=jax.ShapeDtypeStruct((t, width), out_dtype),
        scratch_shapes=[pltpu.VMEM((tm, d_model), BF16)],
        compiler_params=_params(("parallel", "arbitrary")),
        name="in_proj",
    )(h, gain.reshape(1, d_model), w)


def _attention_kernel(*refs, has_prev):
    if has_prev:
        q_ref, kc_ref, kp_ref, vc_ref, vp_ref, o_ref, lse_ref, s_ref = refs
    else:
        q_ref, kc_ref, vc_ref, o_ref, lse_ref, s_ref = refs
    n_keys = s_ref.shape[2]
    row = lax.broadcasted_iota(jnp.int32, (ATT_BLOCK, n_keys), 0)
    col = lax.broadcasted_iota(jnp.int32, (ATT_BLOCK, n_keys), 1)
    if has_prev:
        first_block = pl.program_id(2) == 0
        mask = (col >= row) & (col <= row + ATT_BLOCK) & ((col >= ATT_BLOCK) | jnp.logical_not(first_block))
    else:
        mask = col <= row
    nt = (((1,), (1,)), ((), ()))
    heads = [slice(hd * HEAD_DIM, (hd + 1) * HEAD_DIM) for hd in range(ATT_HEADS)]

    def both(prev_ref, cur_ref, sl):
        if has_prev:
            return jnp.concatenate([prev_ref[:, sl], cur_ref[:, sl]], axis=0)
        return cur_ref[:, sl]

    maxes = []
    for hd, sl in enumerate(heads):
        s = lax.dot_general(q_ref[:, sl], both(kp_ref if has_prev else None, kc_ref, sl), nt,
                            preferred_element_type=F32)
        s = jnp.where(mask, s, MASK_VALUE)
        s_ref[hd] = s
        maxes.append(jnp.max(s, axis=-1, keepdims=True))
    lane = lax.broadcasted_iota(jnp.int32, (ATT_BLOCK, HEAD_DIM), 1)
    lse_tile = jnp.zeros((ATT_BLOCK, HEAD_DIM), F32)
    for hd, sl in enumerate(heads):
        p = jnp.exp(s_ref[hd] - maxes[hd])
        l = jnp.sum(p, axis=-1, keepdims=True)
        acc = jnp.dot(p.astype(BF16), both(vp_ref if has_prev else None, vc_ref, sl),
                      preferred_element_type=F32)
        o_ref[:, sl] = (acc / l).astype(o_ref.dtype)
        lse_tile = jnp.where(lane == hd, maxes[hd] + jnp.log(l), lse_tile)
    lse_ref[...] = lse_tile


def _attention(qkv, dilation, batch, seq):
    length = seq // dilation
    nb = length // ATT_BLOCK
    has_prev = nb > 1
    blk_shape = (None, None, ATT_BLOCK, ATT_GROUP_WIDTH)

    def cur(part):
        return pl.BlockSpec(blk_shape, lambda b, r, i: (b, r, i, part))

    def prev(part):
        return pl.BlockSpec(blk_shape, lambda b, r, i: (b, r, jnp.maximum(i - 1, 0), part))

    if has_prev:
        in_specs = [cur(0), cur(1), prev(1), cur(2), prev(2)]
    else:
        in_specs = [cur(0), cur(1), cur(2)]
    n_keys = 2 * ATT_BLOCK if has_prev else ATT_BLOCK
    return pl.pallas_call(
        functools.partial(_attention_kernel, has_prev=has_prev),
        grid=(batch, dilation, nb),
        in_specs=in_specs,
        out_specs=[pl.BlockSpec(blk_shape, lambda b, r, i: (b, r, i, 0)),
                   pl.BlockSpec((None, None, ATT_BLOCK, HEAD_DIM), lambda b, r, i: (b, r, i, 0))],
        out_shape=[jax.ShapeDtypeStruct((batch, dilation, length, ATT_GROUP_WIDTH), BF16),
                   jax.ShapeDtypeStruct((batch, dilation, length, HEAD_DIM), F32)],
        scratch_shapes=[pltpu.VMEM((ATT_HEADS, ATT_BLOCK, n_keys), F32)],
        compiler_params=_params(("parallel", "parallel", "arbitrary")),
        name="banded_attention",
    )(*([qkv] * len(in_specs)))


def _cast_weight_once(w_ref, wb_ref, first):
    @pl.when(first)
    def _():
        wb_ref[...] = w_ref[...].astype(BF16)


def _merge_out_proj_kernel(o1, o2, o3, l1, l2, l3, h_ref, g_ref, w_ref, out_ref, wb_ref, ot_ref, lt_ref, a_ref):
    _cast_weight_once(w_ref, wb_ref, (pl.program_id(0) == 0) & (pl.program_id(1) == 0))
    tm = h_ref.shape[0]
    for gi, (o_g, l_g) in enumerate(((o2, l2), (o3, l3))):
        dilation = o_g.shape[0]
        n = tm // dilation
        for r in range(dilation):
            rows = pl.ds(r, n, stride=dilation)
            lt_ref[gi, rows, :] = l_g[r]
            for hd in range(ATT_HEADS):
                ot_ref[gi, hd, rows, :] = o_g[r, :, hd * HEAD_DIM:(hd + 1) * HEAD_DIM].astype(F32)
    lses = (l1[0], lt_ref[0], lt_ref[1])
    m = jnp.maximum(jnp.maximum(lses[0], lses[1]), lses[2])
    es = [jnp.exp(v - m) for v in lses]
    den = es[0] + es[1] + es[2]
    ws = [e / den for e in es]
    for hd in range(ATT_HEADS):
        sl = slice(hd * HEAD_DIM, (hd + 1) * HEAD_DIM)
        acc = ws[0][:, hd:hd + 1] * o1[0, :, sl].astype(F32)
        acc = acc + ws[1][:, hd:hd + 1] * ot_ref[0, hd]
        acc = acc + ws[2][:, hd:hd + 1] * ot_ref[1, hd]
        a_ref[:, sl] = acc.astype(BF16)
    y = jnp.dot(a_ref[...], wb_ref[...], preferred_element_type=F32)
    out_ref[...] = h_ref[...] + _rms_scale(y, g_ref[...])


def _merge_out_proj(h, gain, w, layer, outs, lses, batch, seq):
    t, d_model = h.shape
    k = w.shape[1]
    tm = TOKEN_TILE
    tiles = seq // tm
    token_tile = lambda b, i: (b * tiles + i, 0)
    fixed = lambda b, i: (0, 0)

    def group_spec(arr):
        dilation, width = arr.shape[1], arr.shape[3]
        return pl.BlockSpec((None, dilation, tm // dilation, width), lambda b, i: (b, 0, i, 0))

    in_specs = [group_spec(a) for a in outs] + [group_spec(a) for a in lses]
    in_specs += [pl.BlockSpec((tm, d_model), token_tile), pl.BlockSpec((1, d_model), fixed),
                 pl.BlockSpec((None, k, d_model), lambda b, i: (layer, 0, 0), pipeline_mode=SINGLE)]
    return pl.pallas_call(
        _merge_out_proj_kernel,
        grid=(batch, tiles),
        in_specs=in_specs,
        out_specs=pl.BlockSpec((tm, d_model), token_tile),
        out_shape=jax.ShapeDtypeStruct((t, d_model), F32),
        scratch_shapes=[pltpu.VMEM((k, d_model), BF16),
                        pltpu.VMEM((2, ATT_HEADS, tm, HEAD_DIM), F32), pltpu.VMEM((2, tm, HEAD_DIM), F32),
                        pltpu.VMEM((tm, k), BF16)],
        compiler_params=_params(("arbitrary", "arbitrary")),
        name="merge_out_proj",
    )(*outs, *lses, h, gain.reshape(1, d_model), w)


def _out_proj_kernel(a_ref, h_ref, g_ref, w_ref, out_ref, wb_ref):
    _cast_weight_once(w_ref, wb_ref, pl.program_id(0) == 0)
    y = jnp.dot(a_ref[...], wb_ref[...], preferred_element_type=F32)
    out_ref[...] = h_ref[...] + _rms_scale(y, g_ref[...])


def _out_proj(h, gain, w, layer, mixed):
    t, d_model = h.shape
    k = w.shape[1]
    tm = TOKEN_TILE
    row = lambda i: (i, 0)
    fixed = lambda i: (0, 0)
    return pl.pallas_call(
        _out_proj_kernel,
        grid=(t // tm,),
        in_specs=[pl.BlockSpec((tm, k), row), pl.BlockSpec((tm, d_model), row),
                  pl.BlockSpec((1, d_model), fixed),
                  pl.BlockSpec((None, k, d_model), lambda i: (layer, 0, 0), pipeline_mode=SINGLE)],
        out_specs=pl.BlockSpec((tm, d_model), row),
        out_shape=jax.ShapeDtypeStruct((t, d_model), F32),
        scratch_shapes=[pltpu.VMEM((k, d_model), BF16)],
        compiler_params=_params(("arbitrary",)),
        name="out_proj",
    )(mixed, h, gain.reshape(1, d_model), w)


def _mlp_kernel(h_ref, g_in_ref, g_out_ref, w1_ref, w2_ref, out_ref, un_ref, acc_ref):
    f = pl.program_id(1)

    @pl.when(f == 0)
    def _():
        un_ref[...] = _rms_scale(h_ref[...], g_in_ref[...]).astype(BF16)
        acc_ref[...] = jnp.zeros_like(acc_ref)

    a = jnp.dot(un_ref[...], w1_ref[...].astype(BF16), preferred_element_type=F32)
    a = jnp.square(jnp.maximum(a, 0.0))
    acc_ref[...] += jnp.dot(a.astype(BF16), w2_ref[...].astype(BF16), preferred_element_type=F32)

    @pl.when(f == pl.num_programs(1) - 1)
    def _():
        out_ref[...] = h_ref[...] + _rms_scale(acc_ref[...], g_out_ref[...])


def _mlp(h, g_in, g_out, w1, w2, layer):
    t, d_model = h.shape
    d_ff = w1.shape[2]
    tm, tf = MLP_TILE_M, MLP_TILE_F
    return pl.pallas_call(
        _mlp_kernel,
        grid=(t // tm, d_ff // tf),
        in_specs=[pl.BlockSpec((tm, d_model), lambda m, f: (m, 0), pipeline_mode=SINGLE),
                  pl.BlockSpec((1, d_model), lambda m, f: (0, 0)),
                  pl.BlockSpec((1, d_model), lambda m, f: (0, 0)),
                  pl.BlockSpec((None, d_model, tf), lambda m, f: (layer, 0, f)),
                  pl.BlockSpec((None, tf, d_model), lambda m, f: (layer, f, 0))],
        out_specs=pl.BlockSpec((tm, d_model), lambda m, f: (m, 0), pipeline_mode=SINGLE),
        out_shape=jax.ShapeDtypeStruct((t, d_model), F32),
        scratch_shapes=[pltpu.VMEM((tm, d_model), BF16), pltpu.VMEM((tm, d_model), F32)],
        compiler_params=_params(("parallel", "arbitrary")),
        name="mlp",
    )(h, g_in.reshape(1, d_model), g_out.reshape(1, d_model), w1, w2)


def _hgrn_constants():
    r = np.arange(HGRN_CHUNK)[:, None]
    c = np.arange(HGRN_CHUNK)[None, :]
    tri = (c <= r).astype(np.float32)
    msb = np.floor(np.log2(np.maximum(r ^ c, 1))).astype(np.int32)
    lvl = np.where(c < r, msb, -1).astype(np.int32)
    sgn = np.concatenate([np.where((r >> j) & 1 == 1, LOG2E, -LOG2E) * np.ones_like(c)
                          for j in range(2, HGRN_LEVELS)], axis=0).astype(np.float32)
    return tri, lvl, sgn


def _hgrn_kernel(q_ref, i_ref, gt_ref, f_ref, lbp_ref, gn_ref, tri_ref, lvl_ref, sgn_ref, o_ref, st_ref, *,
                 layer_j):
    nh = HGRN_HEADS_PER_STEP
    cs = HGRN_CHUNK
    width = nh * HEAD_DIM

    @pl.when(pl.program_id(2) == 0)
    def _():
        st_ref[...] = jnp.zeros_like(st_ref)

    lbp = lbp_ref[...]
    e = jnp.exp(lbp - jnp.max(lbp, axis=0, keepdims=True))
    p = e / jnp.sum(e, axis=0, keepdims=True)
    csum = p[0:1]
    for t in range(1, layer_j + 1):
        csum = csum + p[t:t + 1]
    lb = csum - p[0:1]
    lb_floor = jnp.maximum(lb, LB_FLOOR)
    one_minus_lb = 1.0 - lb

    z = f_ref[...]
    en = jnp.exp(-jnp.abs(z))
    rcp = 1.0 / (1.0 + en)
    pos = z >= 0.0
    log_f = jnp.log(lb_floor + one_minus_lb * (jnp.where(pos, 1.0, en) * rcp))
    kk = one_minus_lb * (jnp.where(pos, en, 1.0) * rcp) - (lb_floor - lb)

    g_hi = log_f.astype(BF16)
    r1 = log_f - g_hi.astype(F32)
    g_mid = r1.astype(BF16)
    g_lo = (r1 - g_mid.astype(F32)).astype(BF16)
    tri = tri_ref[...]
    b = (jnp.dot(tri, g_hi, preferred_element_type=F32)
         + jnp.dot(tri, g_mid, preferred_element_type=F32)
         + jnp.dot(tri, g_lo, preferred_element_type=F32))
    b_last = b[cs - 1:cs, :]

    row = lax.broadcasted_iota(jnp.int32, (cs, width), 0)
    up = pltpu.roll(log_f, cs - 1, 0)
    down = pltpu.roll(log_f, 1, 0)
    r4 = row & 3
    level_e = [
        jnp.exp(jnp.where((row & 1) == 1, log_f, 0.0)),
        jnp.exp(jnp.where(r4 == 0, up, jnp.where(r4 == 1, 0.0, jnp.where(r4 == 2, log_f, log_f + down)))),
    ]
    for j in range(2, HGRN_LEVELS):
        half = 1 << j
        nblk = cs // (2 * half)
        b3 = b.reshape(nblk, 2 * half, width)
        mid = jnp.broadcast_to(b3[:, half - 1:half, :], b3.shape).reshape(cs, width)
        sign_log2e = jnp.tile(sgn_ref[(j - 2) * cs:(j - 1) * cs, :], (1, nh))
        level_e.append(jnp.exp2((b - mid) * sign_log2e))
    e_incl = jnp.exp(b)
    e_suffix = jnp.exp(b_last - b)

    lvl = lvl_ref[...]
    level_mask = [lvl == j for j in range(HGRN_LEVELS)]
    nt = (((1,), (1,)), ((), ()))
    tn = (((0,), (0,)), ((), ()))
    for hh in range(nh):
        sl = slice(hh * HEAD_DIM, (hh + 1) * HEAD_DIM)
        qb = q_ref[:, sl]
        qh = qb.astype(F32)
        kh = kk[:, sl]
        kb = kh.astype(BF16)
        vb = i_ref[:, sl]
        vh = vb.astype(F32)
        a_mat = jnp.zeros((cs, cs), F32)
        for j in range(HGRN_LEVELS):
            ej = level_e[j][:, sl].astype(BF16)
            aj = lax.dot_general(qb * ej, kb * ej, nt, preferred_element_type=F32)
            a_mat = jnp.where(level_mask[j], aj, a_mat)
        q_dec = qb * e_incl[:, sl].astype(BF16)
        k_dec = kb * e_suffix[:, sl].astype(BF16)
        st = st_ref[hh]
        inter = lax.dot_general(q_dec, st.astype(BF16), nt, preferred_element_type=F32)
        intra = jnp.dot(a_mat.astype(BF16), vb, preferred_element_type=F32)
        diag = jnp.sum(qh * kh, axis=-1, keepdims=True) * vh
        o = inter + intra + diag
        st_ref[hh] = st * e_incl[cs - 1:cs, sl] + lax.dot_general(vb, k_dec, tn, preferred_element_type=F32)
        gt = gt_ref[:, sl].astype(F32)
        on = _rms_scale(o, gn_ref[...]) * (gt * (1.0 / (1.0 + jnp.exp(-gt))))
        o_ref[:, sl] = on.astype(o_ref.dtype)


def _hgrn_recurrence(qig, fpre, lower_bound_params, layer_j, out_norm_gain, batch, seq, d_model):
    wb = HGRN_HEADS_PER_STEP * HEAD_DIM
    hb = d_model // wb
    tri, lvl, sgn = _hgrn_constants()
    n_layers = lower_bound_params.shape[0]

    def part(k):
        return pl.BlockSpec((None, HGRN_CHUNK, wb), lambda b, g, c: (b, c, k * hb + g))

    fixed = lambda b, g, c: (0, 0)
    out = pl.pallas_call(
        functools.partial(_hgrn_kernel, layer_j=layer_j),
        grid=(batch, hb, seq // HGRN_CHUNK),
        in_specs=[part(0), part(1), part(2), part(0),
                  pl.BlockSpec((n_layers, wb), lambda b, g, c: (0, g)),
                  pl.BlockSpec((1, HEAD_DIM), fixed),
                  pl.BlockSpec(tri.shape, fixed),
                  pl.BlockSpec(lvl.shape, fixed),
                  pl.BlockSpec(sgn.shape, fixed)],
        out_specs=pl.BlockSpec((None, HGRN_CHUNK, wb), lambda b, g, c: (b, c, g)),
        out_shape=jax.ShapeDtypeStruct((batch, seq, d_model), BF16),
        scratch_shapes=[pltpu.VMEM((HGRN_HEADS_PER_STEP, HEAD_DIM, HEAD_DIM), F32)],
        compiler_params=_params(("parallel", "parallel", "arbitrary")),
        name="hgrn_recurrence",
    )(qig, qig, qig, fpre, lower_bound_params, out_norm_gain.reshape(1, HEAD_DIM),
      jnp.asarray(tri, BF16), jnp.asarray(lvl), jnp.asarray(sgn))
    return out.reshape(batch * seq, d_model)


@jax.jit
def kernel(x, positions, norm_gains, w_att_in, w_att_out, w_rec_in, rec_lower_bounds, rec_out_norm,
           w_rec_out, w_ff1, w_ff2):
    batch, seq, d_model = x.shape
    depth = norm_gains.shape[0]
    tables = [_rope_tables(positions, dilation) for _, dilation in DILATED_GROUPS]
    h = x.reshape(batch * seq, d_model)
    for layer in range(depth):
        g = norm_gains[layer]
        j = layer // 2
        if layer % 2 == 0:
            outs, lses = [], []
            for gi, (window, dilation) in enumerate(DILATED_GROUPS):
                assert window // dilation == ATT_BLOCK
                qkv = _att_in_proj(h, g[0], w_att_in, j, gi, dilation, batch, seq, tables[gi])
                o, lse = _attention(qkv, dilation, batch, seq)
                outs.append(o)
                lses.append(lse)
            h = _merge_out_proj(h, g[1], w_att_out, j, outs, lses, batch, seq)
        else:
            per_part = d_model // PROJ_TILE_N
            qig = _in_proj(h, g[0], w_rec_in, j, lambda n: n + per_part * (n >= per_part), 3 * d_model, BF16)
            fpre = _in_proj(h, g[0], w_rec_in, j, lambda n: n + per_part, d_model, F32)
            mixed = _hgrn_recurrence(qig.reshape(batch, seq, 3 * d_model), fpre.reshape(batch, seq, d_model),
                                     rec_lower_bounds, j, rec_out_norm[j], batch, seq, d_model)
            h = _out_proj(h, g[1], w_rec_out, j, mixed)
        h = _mlp(h, g[2], g[3], w_ff1, w_ff2, layer)
    return h.reshape(batch, seq, d_model)
```

```python
import functools

import numpy as np
import jax
import jax.numpy as jnp
from jax import lax
from jax.experimental import pallas as pl
from jax.experimental.pallas import tpu as pltpu

F32 = jnp.float32
BF16 = jnp.bfloat16

NORM_EPS = 1e-6
MASK_VALUE = -1e30
LB_FLOOR = 1e-30

DILATED_GROUPS = ((128, 1), (512, 4), (2048, 16))
HEAD_DIM = 128
ATT_HEADS = 8
ATT_BLOCK = 128
ATT_GROUP_WIDTH = ATT_HEADS * HEAD_DIM
ROPE_THETA = 500000.0
ROPE_DIM = HEAD_DIM // 4
ROPE_HALF = ROPE_DIM // 2
TOKEN_TILE = 256
PROJ_TILE_M, PROJ_TILE_N = 2048, 512
OUT_TILE, OUT_SUB = 512, 256
MLP_TILE_M, MLP_TILE_F = 1024, 512
HGRN_CHUNK = 128
HGRN_LEVELS = 7
HGRN_HEADS_PER_STEP = 8
LOG2E = 1.4426950408889634

VMEM_LIMIT_BYTES = 56 * 1024 * 1024
SINGLE = pl.Buffered(1)


def _params(semantics):
    return pltpu.CompilerParams(dimension_semantics=semantics, vmem_limit_bytes=VMEM_LIMIT_BYTES)


def _rms_scale(x, gain):
    ms = jnp.mean(x * x, axis=-1, keepdims=True)
    return x * lax.rsqrt(ms + NORM_EPS) * gain


def _rope_kernel(pos_ref, invf_ref, cos_ref, sa_ref, sb_ref):
    ang = pos_ref[...].astype(F32) * invf_ref[...]
    lane = lax.broadcasted_iota(jnp.int32, ang.shape, 1)
    c = jnp.cos(ang)
    s = jnp.sin(ang)
    cos_ref[...] = jnp.where(lane < ROPE_DIM, c, 1.0)
    sa_ref[...] = jnp.where((lane >= ROPE_HALF) & (lane < ROPE_DIM), s, 0.0)
    sb_ref[...] = jnp.where(lane < ROPE_HALF, -s, 0.0)


def _rope_tables(positions, dilation):
    batch, seq = positions.shape
    t = batch * seq
    tr = 1024
    pos = positions.reshape(batch, seq // dilation, dilation).transpose(0, 2, 1)
    inv_freq = ROPE_THETA ** (-jnp.arange(ROPE_HALF, dtype=F32) / ROPE_HALF)
    invf = jnp.zeros((1, HEAD_DIM), F32).at[0, :ROPE_DIM].set(jnp.tile(inv_freq, 2))
    out = jax.ShapeDtypeStruct((t, HEAD_DIM), F32)
    tabs = pl.pallas_call(
        _rope_kernel,
        grid=(t // tr,),
        in_specs=[pl.BlockSpec((tr, 1), lambda i: (i, 0)),
                  pl.BlockSpec((1, HEAD_DIM), lambda i: (0, 0))],
        out_specs=[pl.BlockSpec((tr, HEAD_DIM), lambda i: (i, 0))] * 3,
        out_shape=[out, out, out],
        compiler_params=_params(("parallel",)),
        name="rope_tables",
    )(pos.reshape(t, 1), invf)
    return [tab.reshape(batch, dilation, seq // dilation, HEAD_DIM) for tab in tabs]


def _att_in_proj_kernel(*refs, dilation):
    refs = list(refs)
    planes_ref = refs.pop() if dilation > 1 else None
    x_ref, g_ref, w_ref, cos_ref, sa_ref, sb_ref, o_ref, wb_ref, xn_ref = refs
    tm, d_model = x_ref.shape
    n = tm // dilation

    @pl.when((pl.program_id(0) == 0) & (pl.program_id(1) == 0))
    def _():
        def cast_rows(i, carry):
            rows = pl.ds(pl.multiple_of(i * HEAD_DIM, HEAD_DIM), HEAD_DIM)
            wb_ref[rows, :] = w_ref[rows, :].astype(BF16)
            return carry
        lax.fori_loop(0, d_model // HEAD_DIM, cast_rows, 0)

    xn = _rms_scale(x_ref[...], g_ref[...])
    if dilation == 1:
        xn_ref[...] = xn.astype(BF16)
    else:
        for c in range(d_model // HEAD_DIM):
            planes_ref[c] = xn[:, c * HEAD_DIM:(c + 1) * HEAD_DIM]
        for r in range(dilation):
            for c in range(d_model // HEAD_DIM):
                rows = planes_ref[c, pl.ds(r, n, stride=dilation), :]
                xn_ref[r * n:(r + 1) * n, c * HEAD_DIM:(c + 1) * HEAD_DIM] = rows.astype(BF16)

    acc = jnp.dot(xn_ref[...], wb_ref[...], preferred_element_type=F32)

    def store(lo, val):
        for r in range(dilation):
            o_ref[r, :, lo:lo + val.shape[1]] = val[r * n:(r + 1) * n].astype(o_ref.dtype)

    cos, sa, sb = (ref[...].reshape(tm, HEAD_DIM) for ref in (cos_ref, sa_ref, sb_ref))
    scale = HEAD_DIM ** -0.5
    for part, mult in ((0, scale), (1, 1.0)):
        c, a, b = cos * mult, sa * mult, sb * mult
        for hd in range(ATT_HEADS):
            lo = part * ATT_GROUP_WIDTH + hd * HEAD_DIM
            xh = acc[:, lo:lo + HEAD_DIM]
            rot = (xh * c + pltpu.roll(xh, ROPE_HALF, 1) * a
                   + pltpu.roll(xh, HEAD_DIM - ROPE_HALF, 1) * b)
            store(lo, rot)
    store(2 * ATT_GROUP_WIDTH, acc[:, 2 * ATT_GROUP_WIDTH:])


def _att_in_proj(h, gain, w, layer, group, dilation, batch, seq, tables):
    t, d_model = h.shape
    tm = TOKEN_TILE
    tiles = seq // tm
    n = tm // dilation
    width = 3 * ATT_GROUP_WIDTH
    residue_tile = lambda b, i: (b, 0, i, 0)
    scratch = [pltpu.VMEM((d_model, width), BF16), pltpu.VMEM((tm, d_model), BF16)]
    if dilation > 1:
        scratch.append(pltpu.VMEM((d_model // HEAD_DIM, tm, HEAD_DIM), F32))
    return pl.pallas_call(
        functools.partial(_att_in_proj_kernel, dilation=dilation),
        grid=(batch, tiles),
        in_specs=[pl.BlockSpec((tm, d_model), lambda b, i: (b * tiles + i, 0)),
                  pl.BlockSpec((1, d_model), lambda b, i: (0, 0)),
                  pl.BlockSpec((None, d_model, width), lambda b, i: (layer, 0, group), pipeline_mode=SINGLE)]
                 + [pl.BlockSpec((None, dilation, n, HEAD_DIM), residue_tile)] * 3,
        out_specs=pl.BlockSpec((None, dilation, n, width), residue_tile),
        out_shape=jax.ShapeDtypeStruct((batch, dilation, seq // dilation, width), BF16),
        scratch_shapes=scratch,
        compiler_params=_params(("arbitrary", "arbitrary")),
        name="att_in_proj",
    )(h, gain.reshape(1, d_model), w, *tables)


def _in_proj_kernel(x_ref, g_ref, w_ref, o_ref, xn_ref):
    @pl.when(pl.program_id(1) == 0)
    def _():
        xn_ref[...] = _rms_scale(x_ref[...], g_ref[...]).astype(BF16)

    acc = jnp.dot(xn_ref[...], w_ref[...].astype(BF16), preferred_element_type=F32)
    o_ref[...] = acc.astype(o_ref.dtype)


def _in_proj(h, gain, w, layer, col_block, width, out_dtype):
    t, d_model = h.shape
    tm, tn = PROJ_TILE_M, PROJ_TILE_N
    return pl.pallas_call(
        _in_proj_kernel,
        grid=(t // tm, width // tn),
        in_specs=[pl.BlockSpec((tm, d_model), lambda i, j: (i, 0), pipeline_mode=SINGLE),
                  pl.BlockSpec((1, d_model), lambda i, j: (0, 0)),
                  pl.BlockSpec((None, d_model, tn), lambda i, j: (layer, 0, col_block(j)))],
        out_specs=pl.BlockSpec((tm, tn), lambda i, j: (i, j)),
        out_shape=jax.ShapeDtypeStruct((t, width), out_dtype),
        scratch_shapes=[pltpu.VMEM((tm, d_model), BF16)],
        compiler_params=_params(("parallel", "arbitrary")),
        name="in_proj",
    )(h, gain.reshape(1, d_model), w)


def _attention_kernel(*refs, has_prev):
    if has_prev:
        q_ref, kc_ref, kp_ref, vc_ref, vp_ref, o_ref, lse_ref, s_ref = refs
    else:
        q_ref, kc_ref, vc_ref, o_ref, lse_ref, s_ref = refs
    n_keys = s_ref.shape[2]
    row = lax.broadcasted_iota(jnp.int32, (ATT_BLOCK, n_keys), 0)
    col = lax.broadcasted_iota(jnp.int32, (ATT_BLOCK, n_keys), 1)
    if has_prev:
        first_block = pl.program_id(2) == 0
        mask = (col >= row) & (col <= row + ATT_BLOCK) & ((col >= ATT_BLOCK) | jnp.logical_not(first_block))
    else:
        mask = col <= row
    nt = (((1,), (1,)), ((), ()))
    heads = [slice(hd * HEAD_DIM, (hd + 1) * HEAD_DIM) for hd in range(ATT_HEADS)]

    def both(prev_ref, cur_ref, sl):
        if has_prev:
            return jnp.concatenate([prev_ref[:, sl], cur_ref[:, sl]], axis=0)
        return cur_ref[:, sl]

    maxes = []
    for hd, sl in enumerate(heads):
        s = lax.dot_general(q_ref[:, sl], both(kp_ref if has_prev else None, kc_ref, sl), nt,
                            preferred_element_type=F32)
        s = jnp.where(mask, s, MASK_VALUE)
        s_ref[hd] = s
        maxes.append(jnp.max(s, axis=-1, keepdims=True))
    lane = lax.broadcasted_iota(jnp.int32, (ATT_BLOCK, HEAD_DIM), 1)
    lse_tile = jnp.zeros((ATT_BLOCK, HEAD_DIM), F32)
    for hd, sl in enumerate(heads):
        p = jnp.exp(s_ref[hd] - maxes[hd])
        l = jnp.sum(p, axis=-1, keepdims=True)
        acc = jnp.dot(p.astype(BF16), both(vp_ref if has_prev else None, vc_ref, sl),
                      preferred_element_type=F32)
        o_ref[:, sl] = (acc / l).astype(o_ref.dtype)
        lse_tile = jnp.where(lane == hd, maxes[hd] + jnp.log(l), lse_tile)
    lse_ref[...] = lse_tile


def _attention(qkv, dilation, batch, seq):
    length = seq // dilation
    nb = length // ATT_BLOCK
    has_prev = nb > 1
    blk_shape = (None, None, ATT_BLOCK, ATT_GROUP_WIDTH)

    def cur(part):
        return pl.BlockSpec(blk_shape, lambda b, r, i: (b, r, i, part))

    def prev(part):
        return pl.BlockSpec(blk_shape, lambda b, r, i: (b, r, jnp.maximum(i - 1, 0), part))

    if has_prev:
        in_specs = [cur(0), cur(1), prev(1), cur(2), prev(2)]
    else:
        in_specs = [cur(0), cur(1), cur(2)]
    n_keys = 2 * ATT_BLOCK if has_prev else ATT_BLOCK
    return pl.pallas_call(
        functools.partial(_attention_kernel, has_prev=has_prev),
        grid=(batch, dilation, nb),
        in_specs=in_specs,
        out_specs=[pl.BlockSpec(blk_shape, lambda b, r, i: (b, r, i, 0)),
                   pl.BlockSpec((None, None, ATT_BLOCK, HEAD_DIM), lambda b, r, i: (b, r, i, 0))],
        out_shape=[jax.ShapeDtypeStruct((batch, dilation, length, ATT_GROUP_WIDTH), BF16),
                   jax.ShapeDtypeStruct((batch, dilation, length, HEAD_DIM), F32)],
        scratch_shapes=[pltpu.VMEM((ATT_HEADS, ATT_BLOCK, n_keys), F32)],
        compiler_params=_params(("parallel", "parallel", "arbitrary")),
        name="banded_attention",
    )(*([qkv] * len(in_specs)))


def _cast_weight_once(w_ref, wb_ref, first):
    @pl.when(first)
    def _():
        wb_ref[...] = w_ref[...].astype(BF16)


def _project_norm_residual(a_ref, wb_ref, h_ref, g_ref, out_ref):
    for lo in range(0, a_ref.shape[0], OUT_SUB):
        rows = slice(lo, lo + OUT_SUB)
        y = jnp.dot(a_ref[rows, :], wb_ref[...], preferred_element_type=F32)
        out_ref[rows, :] = h_ref[rows, :] + _rms_scale(y, g_ref[...])


def _merge_out_proj_kernel(o1, o2, o3, l1, l2, l3, h_ref, g_ref, w_ref, out_ref, wb_ref, ot_ref, lt_ref, a_ref):
    _cast_weight_once(w_ref, wb_ref, (pl.program_id(0) == 0) & (pl.program_id(1) == 0))
    tm = h_ref.shape[0]
    for gi, (o_g, l_g) in enumerate(((o2, l2), (o3, l3))):
        dilation = o_g.shape[0]
        n = tm // dilation
        for r in range(dilation):
            rows = pl.ds(r, n, stride=dilation)
            lt_ref[gi, rows, :] = l_g[r]
            for hd in range(ATT_HEADS):
                ot_ref[gi, hd, rows, :] = o_g[r, :, hd * HEAD_DIM:(hd + 1) * HEAD_DIM].astype(F32)
    lses = (l1[0], lt_ref[0], lt_ref[1])
    m = jnp.maximum(jnp.maximum(lses[0], lses[1]), lses[2])
    es = [jnp.exp(v - m) for v in lses]
    den = es[0] + es[1] + es[2]
    ws = [e / den for e in es]
    for hd in range(ATT_HEADS):
        sl = slice(hd * HEAD_DIM, (hd + 1) * HEAD_DIM)
        acc = ws[0][:, hd:hd + 1] * o1[0, :, sl].astype(F32)
        acc = acc + ws[1][:, hd:hd + 1] * ot_ref[0, hd]
        acc = acc + ws[2][:, hd:hd + 1] * ot_ref[1, hd]
        a_ref[:, sl] = acc.astype(BF16)
    _project_norm_residual(a_ref, wb_ref, h_ref, g_ref, out_ref)


def _merge_out_proj(h, gain, w, layer, outs, lses, batch, seq):
    t, d_model = h.shape
    k = w.shape[1]
    tm = OUT_TILE
    tiles = seq // tm
    token_tile = lambda b, i: (b * tiles + i, 0)
    fixed = lambda b, i: (0, 0)

    def group_spec(arr):
        dilation, width = arr.shape[1], arr.shape[3]
        return pl.BlockSpec((None, dilation, tm // dilation, width), lambda b, i: (b, 0, i, 0))

    in_specs = [group_spec(a) for a in outs] + [group_spec(a) for a in lses]
    in_specs += [pl.BlockSpec((tm, d_model), token_tile), pl.BlockSpec((1, d_model), fixed),
                 pl.BlockSpec((None, k, d_model), lambda b, i: (layer, 0, 0), pipeline_mode=SINGLE)]
    return pl.pallas_call(
        _merge_out_proj_kernel,
        grid=(batch, tiles),
        in_specs=in_specs,
        out_specs=pl.BlockSpec((tm, d_model), token_tile),
        out_shape=jax.ShapeDtypeStruct((t, d_model), F32),
        scratch_shapes=[pltpu.VMEM((k, d_model), BF16),
                        pltpu.VMEM((2, ATT_HEADS, tm, HEAD_DIM), F32), pltpu.VMEM((2, tm, HEAD_DIM), F32),
                        pltpu.VMEM((tm, k), BF16)],
        compiler_params=_params(("arbitrary", "arbitrary")),
        name="merge_out_proj",
    )(*outs, *lses, h, gain.reshape(1, d_model), w)


def _out_proj_kernel(a_ref, h_ref, g_ref, w_ref, out_ref, wb_ref):
    _cast_weight_once(w_ref, wb_ref, pl.program_id(0) == 0)
    _project_norm_residual(a_ref, wb_ref, h_ref, g_ref, out_ref)


def _out_proj(h, gain, w, layer, mixed):
    t, d_model = h.shape
    k = w.shape[1]
    tm = OUT_TILE
    row = lambda i: (i, 0)
    fixed = lambda i: (0, 0)
    return pl.pallas_call(
        _out_proj_kernel,
        grid=(t // tm,),
        in_specs=[pl.BlockSpec((tm, k), row), pl.BlockSpec((tm, d_model), row),
                  pl.BlockSpec((1, d_model), fixed),
                  pl.BlockSpec((None, k, d_model), lambda i: (layer, 0, 0), pipeline_mode=SINGLE)],
        out_specs=pl.BlockSpec((tm, d_model), row),
        out_shape=jax.ShapeDtypeStruct((t, d_model), F32),
        scratch_shapes=[pltpu.VMEM((k, d_model), BF16)],
        compiler_params=_params(("arbitrary",)),
        name="out_proj",
    )(mixed, h, gain.reshape(1, d_model), w)


def _mlp_kernel(h_ref, g_in_ref, g_out_ref, w1_ref, w2_ref, out_ref, un_ref):
    f = pl.program_id(1)

    @pl.when(f == 0)
    def _():
        un_ref[...] = _rms_scale(h_ref[...], g_in_ref[...]).astype(BF16)
        out_ref[...] = jnp.zeros_like(out_ref)

    a = jnp.dot(un_ref[...], w1_ref[...].astype(BF16), preferred_element_type=F32)
    a = jnp.square(jnp.maximum(a, 0.0))
    out_ref[...] += jnp.dot(a.astype(BF16), w2_ref[...].astype(BF16), preferred_element_type=F32)

    @pl.when(f == pl.num_programs(1) - 1)
    def _():
        out_ref[...] = h_ref[...] + _rms_scale(out_ref[...], g_out_ref[...])


def _mlp(h, g_in, g_out, w1, w2, layer):
    t, d_model = h.shape
    d_ff = w1.shape[2]
    tm, tf = MLP_TILE_M, MLP_TILE_F
    return pl.pallas_call(
        _mlp_kernel,
        grid=(t // tm, d_ff // tf),
        in_specs=[pl.BlockSpec((tm, d_model), lambda m, f: (m, 0), pipeline_mode=SINGLE),
                  pl.BlockSpec((1, d_model), lambda m, f: (0, 0)),
                  pl.BlockSpec((1, d_model), lambda m, f: (0, 0)),
                  pl.BlockSpec((None, d_model, tf), lambda m, f: (layer, 0, f)),
                  pl.BlockSpec((None, tf, d_model), lambda m, f: (layer, f, 0))],
        out_specs=pl.BlockSpec((tm, d_model), lambda m, f: (m, 0)),
        out_shape=jax.ShapeDtypeStruct((t, d_model), F32),
        scratch_shapes=[pltpu.VMEM((tm, d_model), BF16)],
        compiler_params=_params(("parallel", "arbitrary")),
        name="mlp",
    )(h, g_in.reshape(1, d_model), g_out.reshape(1, d_model), w1, w2)


def _hgrn_constants():
    r = np.arange(HGRN_CHUNK)[:, None]
    c = np.arange(HGRN_CHUNK)[None, :]
    tri = (c <= r).astype(np.float32)
    msb = np.floor(np.log2(np.maximum(r ^ c, 1))).astype(np.int32)
    lvl = np.where(c < r, msb, -1).astype(np.int32)
    sgn = np.concatenate([np.where((r >> j) & 1 == 1, LOG2E, -LOG2E) * np.ones_like(c)
                          for j in range(2, HGRN_LEVELS)], axis=0).astype(np.float32)
    return tri, lvl, sgn


def _hgrn_kernel(q_ref, i_ref, gt_ref, f_ref, lbp_ref, gn_ref, tri_ref, lvl_ref, sgn_ref, o_ref, st_ref, *,
                 layer_j):
    nh = HGRN_HEADS_PER_STEP
    cs = HGRN_CHUNK
    width = nh * HEAD_DIM

    @pl.when(pl.program_id(2) == 0)
    def _():
        st_ref[...] = jnp.zeros_like(st_ref)

    lbp = lbp_ref[...]
    e = jnp.exp(lbp - jnp.max(lbp, axis=0, keepdims=True))
    p = e / jnp.sum(e, axis=0, keepdims=True)
    csum = p[0:1]
    for t in range(1, layer_j + 1):
        csum = csum + p[t:t + 1]
    lb = csum - p[0:1]
    lb_floor = jnp.maximum(lb, LB_FLOOR)
    one_minus_lb = 1.0 - lb

    z = f_ref[...]
    en = jnp.exp(-jnp.abs(z))
    rcp = 1.0 / (1.0 + en)
    pos = z >= 0.0
    log_f = jnp.log(lb_floor + one_minus_lb * (jnp.where(pos, 1.0, en) * rcp))
    kk = one_minus_lb * (jnp.where(pos, en, 1.0) * rcp) - (lb_floor - lb)

    g_hi = log_f.astype(BF16)
    r1 = log_f - g_hi.astype(F32)
    g_mid = r1.astype(BF16)
    g_lo = (r1 - g_mid.astype(F32)).astype(BF16)
    tri = tri_ref[...]
    b = (jnp.dot(tri, g_hi, preferred_element_type=F32)
         + jnp.dot(tri, g_mid, preferred_element_type=F32)
         + jnp.dot(tri, g_lo, preferred_element_type=F32))
    b_last = b[cs - 1:cs, :]

    row = lax.broadcasted_iota(jnp.int32, (cs, width), 0)
    up = pltpu.roll(log_f, cs - 1, 0)
    down = pltpu.roll(log_f, 1, 0)
    r4 = row & 3
    level_e = [
        jnp.exp(jnp.where((row & 1) == 1, log_f, 0.0)),
        jnp.exp(jnp.where(r4 == 0, up, jnp.where(r4 == 1, 0.0, jnp.where(r4 == 2, log_f, log_f + down)))),
    ]
    for j in range(2, HGRN_LEVELS):
        half = 1 << j
        nblk = cs // (2 * half)
        b3 = b.reshape(nblk, 2 * half, width)
        mid = jnp.broadcast_to(b3[:, half - 1:half, :], b3.shape).reshape(cs, width)
        sign_log2e = jnp.tile(sgn_ref[(j - 2) * cs:(j - 1) * cs, :], (1, nh))
        level_e.append(jnp.exp2((b - mid) * sign_log2e))
    e_incl = jnp.exp(b)
    e_suffix = jnp.exp(b_last - b)

    lvl = lvl_ref[...]
    level_mask = [lvl == j for j in range(HGRN_LEVELS)]
    nt = (((1,), (1,)), ((), ()))
    tn = (((0,), (0,)), ((), ()))
    for hh in range(nh):
        sl = slice(hh * HEAD_DIM, (hh + 1) * HEAD_DIM)
        qb = q_ref[:, sl]
        qh = qb.astype(F32)
        kh = kk[:, sl]
        kb = kh.astype(BF16)
        vb = i_ref[:, sl]
        vh = vb.astype(F32)
        a_mat = jnp.zeros((cs, cs), F32)
        for j in range(HGRN_LEVELS):
            ej = level_e[j][:, sl].astype(BF16)
            aj = lax.dot_general(qb * ej, kb * ej, nt, preferred_element_type=F32)
            a_mat = jnp.where(level_mask[j], aj, a_mat)
        q_dec = qb * e_incl[:, sl].astype(BF16)
        k_dec = kb * e_suffix[:, sl].astype(BF16)
        st = st_ref[hh]
        inter = lax.dot_general(q_dec, st.astype(BF16), nt, preferred_element_type=F32)
        intra = jnp.dot(a_mat.astype(BF16), vb, preferred_element_type=F32)
        diag = jnp.sum(qh * kh, axis=-1, keepdims=True) * vh
        o = inter + intra + diag
        st_ref[hh] = st * e_incl[cs - 1:cs, sl] + lax.dot_general(vb, k_dec, tn, preferred_element_type=F32)
        gt = gt_ref[:, sl].astype(F32)
        on = _rms_scale(o, gn_ref[...]) * (gt * (1.0 / (1.0 + jnp.exp(-gt))))
        o_ref[:, sl] = on.astype(o_ref.dtype)


def _hgrn_recurrence(qig, fpre, lower_bound_params, layer_j, out_norm_gain, batch, seq, d_model):
    wb = HGRN_HEADS_PER_STEP * HEAD_DIM
    hb = d_model // wb
    tri, lvl, sgn = _hgrn_constants()
    n_layers = lower_bound_params.shape[0]

    def part(k):
        return pl.BlockSpec((None, HGRN_CHUNK, wb), lambda b, g, c: (b, c, k * hb + g))

    fixed = lambda b, g, c: (0, 0)
    out = pl.pallas_call(
        functools.partial(_hgrn_kernel, layer_j=layer_j),
        grid=(batch, hb, seq // HGRN_CHUNK),
        in_specs=[part(0), part(1), part(2), part(0),
                  pl.BlockSpec((n_layers, wb), lambda b, g, c: (0, g)),
                  pl.BlockSpec((1, HEAD_DIM), fixed),
                  pl.BlockSpec(tri.shape, fixed),
                  pl.BlockSpec(lvl.shape, fixed),
                  pl.BlockSpec(sgn.shape, fixed)],
        out_specs=pl.BlockSpec((None, HGRN_CHUNK, wb), lambda b, g, c: (b, c, g)),
        out_shape=jax.ShapeDtypeStruct((batch, seq, d_model), BF16),
        scratch_shapes=[pltpu.VMEM((HGRN_HEADS_PER_STEP, HEAD_DIM, HEAD_DIM), F32)],
        compiler_params=_params(("parallel", "parallel", "arbitrary")),
        name="hgrn_recurrence",
    )(qig, qig, qig, fpre, lower_bound_params, out_norm_gain.reshape(1, HEAD_DIM),
      jnp.asarray(tri, BF16), jnp.asarray(lvl), jnp.asarray(sgn))
    return out.reshape(batch * seq, d_model)


@jax.jit
def kernel(x, positions, norm_gains, w_att_in, w_att_out, w_rec_in, rec_lower_bounds, rec_out_norm,
           w_rec_out, w_ff1, w_ff2):
    batch, seq, d_model = x.shape
    depth = norm_gains.shape[0]
    tables = [_rope_tables(positions, dilation) for _, dilation in DILATED_GROUPS]
    h = x.reshape(batch * seq, d_model)
    for layer in range(depth):
        g = norm_gains[layer]
        j = layer // 2
        if layer % 2 == 0:
            outs, lses = [], []
            for gi, (window, dilation) in enumerate(DILATED_GROUPS):
                assert window // dilation == ATT_BLOCK
                qkv = _att_in_proj(h, g[0], w_att_in, j, gi, dilation, batch, seq, tables[gi])
                o, lse = _attention(qkv, dilation, batch, seq)
                outs.append(o)
                lses.append(lse)
            h = _merge_out_proj(h, g[1], w_att_out, j, outs, lses, batch, seq)
        else:
            per_part = d_model // PROJ_TILE_N
            qig = _in_proj(h, g[0], w_rec_in, j, lambda n: n + per_part * (n >= per_part), 3 * d_model, BF16)
            fpre = _in_proj(h, g[0], w_rec_in, j, lambda n: n + per_part, d_model, F32)
            mixed = _hgrn_recurrence(qig.reshape(batch, seq, 3 * d_model), fpre.reshape(batch, seq, d_model),
                                     rec_lower_bounds, j, rec_out_norm[j], batch, seq, d_model)
            h = _out_proj(h, g[1], w_rec_out, j, mixed)
        h = _mlp(h, g[2], g[3], w_ff1, w_ff2, layer)
    return h.reshape(batch, seq, d_model)
```

```python
import functools

import numpy as np
import jax
import jax.numpy as jnp
from jax import lax
from jax.experimental import pallas as pl
from jax.experimental.pallas import tpu as pltpu

F32 = jnp.float32
BF16 = jnp.bfloat16

NORM_EPS = 1e-6
MASK_VALUE = -1e30
LB_FLOOR = 1e-30

DILATED_GROUPS = ((128, 1), (512, 4), (2048, 16))
HEAD_DIM = 128
ATT_HEADS = 8
ATT_BLOCK = 128
ATT_GROUP_WIDTH = ATT_HEADS * HEAD_DIM
ROPE_THETA = 500000.0
ROPE_DIM = HEAD_DIM // 4
ROPE_HALF = ROPE_DIM // 2
TOKEN_TILE = 256
PROJ_TILE_M, PROJ_TILE_N = 2048, 512
OUT_TILE, OUT_SUB = 512, 256
MLP_TILE_M, MLP_TILE_F = 1024, 512
HGRN_CHUNK = 128
HGRN_LEVELS = 7
HGRN_HEADS_PER_STEP = 8
LOG2E = 1.4426950408889634

VMEM_LIMIT_BYTES = 56 * 1024 * 1024
SINGLE = pl.Buffered(1)


def _params(semantics):
    return pltpu.CompilerParams(dimension_semantics=semantics, vmem_limit_bytes=VMEM_LIMIT_BYTES)


def _rms_scale(x, gain):
    ms = jnp.mean(x * x, axis=-1, keepdims=True)
    return x * lax.rsqrt(ms + NORM_EPS) * gain


def _rope_kernel(pos_ref, invf_ref, cos_ref, sa_ref, sb_ref):
    ang = pos_ref[...].astype(F32) * invf_ref[...]
    lane = lax.broadcasted_iota(jnp.int32, ang.shape, 1)
    c = jnp.cos(ang)
    s = jnp.sin(ang)
    cos_ref[...] = jnp.where(lane < ROPE_DIM, c, 1.0)
    sa_ref[...] = jnp.where((lane >= ROPE_HALF) & (lane < ROPE_DIM), s, 0.0)
    sb_ref[...] = jnp.where(lane < ROPE_HALF, -s, 0.0)


def _rope_tables(positions, dilation):
    batch, seq = positions.shape
    t = batch * seq
    tr = 1024
    pos = positions.reshape(batch, seq // dilation, dilation).transpose(0, 2, 1)
    inv_freq = ROPE_THETA ** (-jnp.arange(ROPE_HALF, dtype=F32) / ROPE_HALF)
    invf = jnp.zeros((1, HEAD_DIM), F32).at[0, :ROPE_DIM].set(jnp.tile(inv_freq, 2))
    out = jax.ShapeDtypeStruct((t, HEAD_DIM), F32)
    tabs = pl.pallas_call(
        _rope_kernel,
        grid=(t // tr,),
        in_specs=[pl.BlockSpec((tr, 1), lambda i: (i, 0)),
                  pl.BlockSpec((1, HEAD_DIM), lambda i: (0, 0))],
        out_specs=[pl.BlockSpec((tr, HEAD_DIM), lambda i: (i, 0))] * 3,
        out_shape=[out, out, out],
        compiler_params=_params(("parallel",)),
        name="rope_tables",
    )(pos.reshape(t, 1), invf)
    return [tab.reshape(batch, dilation, seq // dilation, HEAD_DIM) for tab in tabs]


def _att_in_proj_kernel(*refs, dilation):
    refs = list(refs)
    planes_ref = refs.pop() if dilation > 1 else None
    x_ref, g_ref, w_ref, cos_ref, sa_ref, sb_ref, o_ref, wb_ref, xn_ref = refs
    tm, d_model = x_ref.shape
    n = tm // dilation

    @pl.when((pl.program_id(0) == 0) & (pl.program_id(1) == 0))
    def _():
        def cast_rows(i, carry):
            rows = pl.ds(pl.multiple_of(i * HEAD_DIM, HEAD_DIM), HEAD_DIM)
            wb_ref[rows, :] = w_ref[rows, :].astype(BF16)
            return carry
        lax.fori_loop(0, d_model // HEAD_DIM, cast_rows, 0)

    xn = _rms_scale(x_ref[...], g_ref[...])
    if dilation == 1:
        xn_ref[...] = xn.astype(BF16)
    else:
        for c in range(d_model // HEAD_DIM):
            planes_ref[c] = xn[:, c * HEAD_DIM:(c + 1) * HEAD_DIM]
        for r in range(dilation):
            for c in range(d_model // HEAD_DIM):
                rows = planes_ref[c, pl.ds(r, n, stride=dilation), :]
                xn_ref[r * n:(r + 1) * n, c * HEAD_DIM:(c + 1) * HEAD_DIM] = rows.astype(BF16)

    acc = jnp.dot(xn_ref[...], wb_ref[...], preferred_element_type=F32)

    def store(lo, val):
        for r in range(dilation):
            o_ref[r, :, lo:lo + val.shape[1]] = val[r * n:(r + 1) * n].astype(o_ref.dtype)

    cos, sa, sb = (ref[...].reshape(tm, HEAD_DIM) for ref in (cos_ref, sa_ref, sb_ref))
    scale = HEAD_DIM ** -0.5
    for part, mult in ((0, scale), (1, 1.0)):
        c, a, b = cos * mult, sa * mult, sb * mult
        for hd in range(ATT_HEADS):
            lo = part * ATT_GROUP_WIDTH + hd * HEAD_DIM
            xh = acc[:, lo:lo + HEAD_DIM]
            rot = (xh * c + pltpu.roll(xh, ROPE_HALF, 1) * a
                   + pltpu.roll(xh, HEAD_DIM - ROPE_HALF, 1) * b)
            store(lo, rot)
    store(2 * ATT_GROUP_WIDTH, acc[:, 2 * ATT_GROUP_WIDTH:])


def _att_in_proj(h, gain, w, layer, group, dilation, batch, seq, tables):
    t, d_model = h.shape
    tm = TOKEN_TILE
    tiles = seq // tm
    n = tm // dilation
    width = 3 * ATT_GROUP_WIDTH
    residue_tile = lambda b, i: (b, 0, i, 0)
    scratch = [pltpu.VMEM((d_model, width), BF16), pltpu.VMEM((tm, d_model), BF16)]
    if dilation > 1:
        scratch.append(pltpu.VMEM((d_model // HEAD_DIM, tm, HEAD_DIM), F32))
    return pl.pallas_call(
        functools.partial(_att_in_proj_kernel, dilation=dilation),
        grid=(batch, tiles),
        in_specs=[pl.BlockSpec((tm, d_model), lambda b, i: (b * tiles + i, 0)),
                  pl.BlockSpec((1, d_model), lambda b, i: (0, 0)),
                  pl.BlockSpec((None, d_model, width), lambda b, i: (layer, 0, group), pipeline_mode=SINGLE)]
                 + [pl.BlockSpec((None, dilation, n, HEAD_DIM), residue_tile)] * 3,
        out_specs=pl.BlockSpec((None, dilation, n, width), residue_tile),
        out_shape=jax.ShapeDtypeStruct((batch, dilation, seq // dilation, width), BF16),
        scratch_shapes=scratch,
        compiler_params=_params(("arbitrary", "arbitrary")),
        name="att_in_proj",
    )(h, gain.reshape(1, d_model), w, *tables)


def _in_proj_kernel(x_ref, g_ref, w_ref, o_ref, xn_ref):
    @pl.when(pl.program_id(1) == 0)
    def _():
        xn_ref[...] = _rms_scale(x_ref[...], g_ref[...]).astype(BF16)

    acc = jnp.dot(xn_ref[...], w_ref[...].astype(BF16), preferred_element_type=F32)
    o_ref[...] = acc.astype(o_ref.dtype)


def _in_proj(h, gain, w, layer, col_block, width, out_dtype):
    t, d_model = h.shape
    tm, tn = PROJ_TILE_M, PROJ_TILE_N
    return pl.pallas_call(
        _in_proj_kernel,
        grid=(t // tm, width // tn),
        in_specs=[pl.BlockSpec((tm, d_model), lambda i, j: (i, 0), pipeline_mode=SINGLE),
                  pl.BlockSpec((1, d_model), lambda i, j: (0, 0)),
                  pl.BlockSpec((None, d_model, tn), lambda i, j: (layer, 0, col_block(j)))],
        out_specs=pl.BlockSpec((tm, tn), lambda i, j: (i, j)),
        out_shape=jax.ShapeDtypeStruct((t, width), out_dtype),
        scratch_shapes=[pltpu.VMEM((tm, d_model), BF16)],
        compiler_params=_params(("parallel", "arbitrary")),
        name="in_proj",
    )(h, gain.reshape(1, d_model), w)


def _attention_kernel(*refs, has_prev):
    if has_prev:
        q_ref, kc_ref, vc_ref, o_ref, lse_ref, s_ref, prev_ref = refs
        kp_ref, vp_ref = prev_ref.at[0], prev_ref.at[1]
        first_block = pl.program_id(2) == 0

        @pl.when(first_block)
        def _():
            prev_ref[...] = jnp.zeros_like(prev_ref)
    else:
        q_ref, kc_ref, vc_ref, o_ref, lse_ref, s_ref = refs
    n_keys = s_ref.shape[2]
    row = lax.broadcasted_iota(jnp.int32, (ATT_BLOCK, n_keys), 0)
    col = lax.broadcasted_iota(jnp.int32, (ATT_BLOCK, n_keys), 1)
    if has_prev:
        mask = (col >= row) & (col <= row + ATT_BLOCK) & ((col >= ATT_BLOCK) | jnp.logical_not(first_block))
    else:
        mask = col <= row
    nt = (((1,), (1,)), ((), ()))
    heads = [slice(hd * HEAD_DIM, (hd + 1) * HEAD_DIM) for hd in range(ATT_HEADS)]

    def both(prev_ref, cur_ref, sl):
        if has_prev:
            return jnp.concatenate([prev_ref[:, sl], cur_ref[:, sl]], axis=0)
        return cur_ref[:, sl]

    maxes = []
    for hd, sl in enumerate(heads):
        s = lax.dot_general(q_ref[:, sl], both(kp_ref if has_prev else None, kc_ref, sl), nt,
                            preferred_element_type=F32)
        s = jnp.where(mask, s, MASK_VALUE)
        s_ref[hd] = s
        maxes.append(jnp.max(s, axis=-1, keepdims=True))
    lane = lax.broadcasted_iota(jnp.int32, (ATT_BLOCK, HEAD_DIM), 1)
    lse_tile = jnp.zeros((ATT_BLOCK, HEAD_DIM), F32)
    for hd, sl in enumerate(heads):
        p = jnp.exp(s_ref[hd] - maxes[hd])
        l = jnp.sum(p, axis=-1, keepdims=True)
        acc = jnp.dot(p.astype(BF16), both(vp_ref if has_prev else None, vc_ref, sl),
                      preferred_element_type=F32)
        o_ref[:, sl] = (acc / l).astype(o_ref.dtype)
        lse_tile = jnp.where(lane == hd, maxes[hd] + jnp.log(l), lse_tile)
    lse_ref[...] = lse_tile
    if has_prev:
        kp_ref[...] = kc_ref[...]
        vp_ref[...] = vc_ref[...]


def _attention(qkv, dilation, batch, seq):
    length = seq // dilation
    nb = length // ATT_BLOCK
    has_prev = nb > 1
    blk_shape = (None, None, ATT_BLOCK, ATT_GROUP_WIDTH)

    def cur(part):
        return pl.BlockSpec(blk_shape, lambda b, r, i: (b, r, i, part))

    in_specs = [cur(0), cur(1), cur(2)]
    n_keys = 2 * ATT_BLOCK if has_prev else ATT_BLOCK
    scratch = [pltpu.VMEM((ATT_HEADS, ATT_BLOCK, n_keys), F32)]
    if has_prev:
        scratch.append(pltpu.VMEM((2, ATT_BLOCK, ATT_GROUP_WIDTH), BF16))
    return pl.pallas_call(
        functools.partial(_attention_kernel, has_prev=has_prev),
        grid=(batch, dilation, nb),
        in_specs=in_specs,
        out_specs=[pl.BlockSpec(blk_shape, lambda b, r, i: (b, r, i, 0)),
                   pl.BlockSpec((None, None, ATT_BLOCK, HEAD_DIM), lambda b, r, i: (b, r, i, 0))],
        out_shape=[jax.ShapeDtypeStruct((batch, dilation, length, ATT_GROUP_WIDTH), BF16),
                   jax.ShapeDtypeStruct((batch, dilation, length, HEAD_DIM), F32)],
        scratch_shapes=scratch,
        compiler_params=_params(("parallel", "parallel", "arbitrary")),
        name="banded_attention",
    )(*([qkv] * len(in_specs)))


def _cast_weight_once(w_ref, wb_ref, first):
    @pl.when(first)
    def _():
        wb_ref[...] = w_ref[...].astype(BF16)


def _project_norm_residual(a_ref, wb_ref, h_ref, g_ref, out_ref):
    for lo in range(0, a_ref.shape[0], OUT_SUB):
        rows = slice(lo, lo + OUT_SUB)
        y = jnp.dot(a_ref[rows, :], wb_ref[...], preferred_element_type=F32)
        out_ref[rows, :] = h_ref[rows, :] + _rms_scale(y, g_ref[...])


def _merge_out_proj_kernel(o1, o2, o3, l1, l2, l3, h_ref, g_ref, w_ref, out_ref, wb_ref, ot_ref, lt_ref, a_ref):
    _cast_weight_once(w_ref, wb_ref, (pl.program_id(0) == 0) & (pl.program_id(1) == 0))
    tm = h_ref.shape[0]
    for gi, (o_g, l_g) in enumerate(((o2, l2), (o3, l3))):
        dilation = o_g.shape[0]
        n = tm // dilation
        for r in range(dilation):
            rows = pl.ds(r, n, stride=dilation)
            lt_ref[gi, rows, :] = l_g[r]
            for hd in range(ATT_HEADS):
                ot_ref[gi, hd, rows, :] = o_g[r, :, hd * HEAD_DIM:(hd + 1) * HEAD_DIM].astype(F32)
    lses = (l1[0], lt_ref[0], lt_ref[1])
    m = jnp.maximum(jnp.maximum(lses[0], lses[1]), lses[2])
    es = [jnp.exp(v - m) for v in lses]
    den = es[0] + es[1] + es[2]
    ws = [e / den for e in es]
    for hd in range(ATT_HEADS):
        sl = slice(hd * HEAD_DIM, (hd + 1) * HEAD_DIM)
        acc = ws[0][:, hd:hd + 1] * o1[0, :, sl].astype(F32)
        acc = acc + ws[1][:, hd:hd + 1] * ot_ref[0, hd]
        acc = acc + ws[2][:, hd:hd + 1] * ot_ref[1, hd]
        a_ref[:, sl] = acc.astype(BF16)
    _project_norm_residual(a_ref, wb_ref, h_ref, g_ref, out_ref)


def _merge_out_proj(h, gain, w, layer, outs, lses, batch, seq):
    t, d_model = h.shape
    k = w.shape[1]
    tm = OUT_TILE
    tiles = seq // tm
    token_tile = lambda b, i: (b * tiles + i, 0)
    fixed = lambda b, i: (0, 0)

    def group_spec(arr):
        dilation, width = arr.shape[1], arr.shape[3]
        return pl.BlockSpec((None, dilation, tm // dilation, width), lambda b, i: (b, 0, i, 0))

    in_specs = [group_spec(a) for a in outs] + [group_spec(a) for a in lses]
    in_specs += [pl.BlockSpec((tm, d_model), token_tile), pl.BlockSpec((1, d_model), fixed),
                 pl.BlockSpec((None, k, d_model), lambda b, i: (layer, 0, 0), pipeline_mode=SINGLE)]
    return pl.pallas_call(
        _merge_out_proj_kernel,
        grid=(batch, tiles),
        in_specs=in_specs,
        out_specs=pl.BlockSpec((tm, d_model), token_tile),
        out_shape=jax.ShapeDtypeStruct((t, d_model), F32),
        scratch_shapes=[pltpu.VMEM((k, d_model), BF16),
                        pltpu.VMEM((2, ATT_HEADS, tm, HEAD_DIM), F32), pltpu.VMEM((2, tm, HEAD_DIM), F32),
                        pltpu.VMEM((tm, k), BF16)],
        compiler_params=_params(("arbitrary", "arbitrary")),
        name="merge_out_proj",
    )(*outs, *lses, h, gain.reshape(1, d_model), w)


def _out_proj_kernel(a_ref, h_ref, g_ref, w_ref, out_ref, wb_ref):
    _cast_weight_once(w_ref, wb_ref, pl.program_id(0) == 0)
    _project_norm_residual(a_ref, wb_ref, h_ref, g_ref, out_ref)


def _out_proj(h, gain, w, layer, mixed):
    t, d_model = h.shape
    k = w.shape[1]
    tm = OUT_TILE
    row = lambda i: (i, 0)
    fixed = lambda i: (0, 0)
    return pl.pallas_call(
        _out_proj_kernel,
        grid=(t // tm,),
        in_specs=[pl.BlockSpec((tm, k), row), pl.BlockSpec((tm, d_model), row),
                  pl.BlockSpec((1, d_model), fixed),
                  pl.BlockSpec((None, k, d_model), lambda i: (layer, 0, 0), pipeline_mode=SINGLE)],
        out_specs=pl.BlockSpec((tm, d_model), row),
        out_shape=jax.ShapeDtypeStruct((t, d_model), F32),
        scratch_shapes=[pltpu.VMEM((k, d_model), BF16)],
        compiler_params=_params(("arbitrary",)),
        name="out_proj",
    )(mixed, h, gain.reshape(1, d_model), w)


def _mlp_kernel(h_ref, g_in_ref, g_out_ref, w1_ref, w2_ref, out_ref, un_ref):
    f = pl.program_id(1)

    @pl.when(f == 0)
    def _():
        un_ref[...] = _rms_scale(h_ref[...], g_in_ref[...]).astype(BF16)
        out_ref[...] = jnp.zeros_like(out_ref)

    a = jnp.dot(un_ref[...], w1_ref[...], preferred_element_type=F32)
    a = jnp.square(jnp.maximum(a, 0.0))
    out_ref[...] += jnp.dot(a.astype(BF16), w2_ref[...].astype(BF16), preferred_element_type=F32)

    @pl.when(f == pl.num_programs(1) - 1)
    def _():
        out_ref[...] = h_ref[...] + _rms_scale(out_ref[...], g_out_ref[...])


def _mlp(h, g_in, g_out, w1, w2, layer):
    t, d_model = h.shape
    d_ff = w1.shape[2]
    tm, tf = MLP_TILE_M, MLP_TILE_F
    return pl.pallas_call(
        _mlp_kernel,
        grid=(t // tm, d_ff // tf),
        in_specs=[pl.BlockSpec((tm, d_model), lambda m, f: (m, 0), pipeline_mode=SINGLE),
                  pl.BlockSpec((1, d_model), lambda m, f: (0, 0)),
                  pl.BlockSpec((1, d_model), lambda m, f: (0, 0)),
                  pl.BlockSpec((None, d_model, tf), lambda m, f: (layer, 0, f)),
                  pl.BlockSpec((None, tf, d_model), lambda m, f: (layer, f, 0))],
        out_specs=pl.BlockSpec((tm, d_model), lambda m, f: (m, 0)),
        out_shape=jax.ShapeDtypeStruct((t, d_model), F32),
        scratch_shapes=[pltpu.VMEM((tm, d_model), BF16)],
        compiler_params=_params(("parallel", "arbitrary")),
        name="mlp",
    )(h, g_in.reshape(1, d_model), g_out.reshape(1, d_model), w1, w2)


def _hgrn_constants():
    r = np.arange(HGRN_CHUNK)[:, None]
    c = np.arange(HGRN_CHUNK)[None, :]
    tri = (c <= r).astype(np.float32)
    msb = np.floor(np.log2(np.maximum(r ^ c, 1))).astype(np.int32)
    lvl = np.where(c < r, msb, -1).astype(np.int32)
    sgn = np.concatenate([np.where((r >> j) & 1 == 1, LOG2E, -LOG2E) * np.ones_like(c)
                          for j in range(2, HGRN_LEVELS)], axis=0).astype(np.float32)
    return tri, lvl, sgn


def _hgrn_kernel(q_ref, i_ref, gt_ref, f_ref, lbp_ref, gn_ref, tri_ref, lvl_ref, sgn_ref, o_ref, st_ref, *,
                 layer_j):
    nh = HGRN_HEADS_PER_STEP
    cs = HGRN_CHUNK
    width = nh * HEAD_DIM

    @pl.when(pl.program_id(2) == 0)
    def _():
        st_ref[...] = jnp.zeros_like(st_ref)

    lbp = lbp_ref[...]
    e = jnp.exp(lbp - jnp.max(lbp, axis=0, keepdims=True))
    p = e / jnp.sum(e, axis=0, keepdims=True)
    csum = p[0:1]
    for t in range(1, layer_j + 1):
        csum = csum + p[t:t + 1]
    lb = csum - p[0:1]
    lb_floor = jnp.maximum(lb, LB_FLOOR)
    one_minus_lb = 1.0 - lb

    z = f_ref[...]
    en = jnp.exp(-jnp.abs(z))
    rcp = 1.0 / (1.0 + en)
    pos = z >= 0.0
    log_f = jnp.log(lb_floor + one_minus_lb * (jnp.where(pos, 1.0, en) * rcp))
    kk = one_minus_lb * (jnp.where(pos, en, 1.0) * rcp) - (lb_floor - lb)

    g_hi = log_f.astype(BF16)
    r1 = log_f - g_hi.astype(F32)
    g_mid = r1.astype(BF16)
    g_lo = (r1 - g_mid.astype(F32)).astype(BF16)
    tri = tri_ref[...]
    b = (jnp.dot(tri, g_hi, preferred_element_type=F32)
         + jnp.dot(tri, g_mid, preferred_element_type=F32)
         + jnp.dot(tri, g_lo, preferred_element_type=F32))
    b_last = b[cs - 1:cs, :]

    row = lax.broadcasted_iota(jnp.int32, (cs, width), 0)
    up = pltpu.roll(log_f, cs - 1, 0)
    down = pltpu.roll(log_f, 1, 0)
    r4 = row & 3
    level_e = [
        jnp.exp(jnp.where((row & 1) == 1, log_f, 0.0)),
        jnp.exp(jnp.where(r4 == 0, up, jnp.where(r4 == 1, 0.0, jnp.where(r4 == 2, log_f, log_f + down)))),
    ]
    for j in range(2, HGRN_LEVELS):
        half = 1 << j
        nblk = cs // (2 * half)
        b3 = b.reshape(nblk, 2 * half, width)
        mid = jnp.broadcast_to(b3[:, half - 1:half, :], b3.shape).reshape(cs, width)
        sign_log2e = jnp.tile(sgn_ref[(j - 2) * cs:(j - 1) * cs, :], (1, nh))
        level_e.append(jnp.exp2((b - mid) * sign_log2e))
    e_incl = jnp.exp(b)
    e_suffix = jnp.exp(b_last - b)

    lvl = lvl_ref[...]
    level_mask = [lvl == j for j in range(HGRN_LEVELS)]
    nt = (((1,), (1,)), ((), ()))
    tn = (((0,), (0,)), ((), ()))
    for hh in range(nh):
        sl = slice(hh * HEAD_DIM, (hh + 1) * HEAD_DIM)
        qb = q_ref[:, sl]
        qh = qb.astype(F32)
        kh = kk[:, sl]
        kb = kh.astype(BF16)
        vb = i_ref[:, sl]
        vh = vb.astype(F32)
        a_mat = jnp.zeros((cs, cs), F32)
        for j in range(HGRN_LEVELS):
            ej = level_e[j][:, sl].astype(BF16)
            aj = lax.dot_general(qb * ej, kb * ej, nt, preferred_element_type=F32)
            a_mat = jnp.where(level_mask[j], aj, a_mat)
        q_dec = qb * e_incl[:, sl].astype(BF16)
        k_dec = kb * e_suffix[:, sl].astype(BF16)
        st = st_ref[hh]
        inter = lax.dot_general(q_dec, st.astype(BF16), nt, preferred_element_type=F32)
        intra = jnp.dot(a_mat.astype(BF16), vb, preferred_element_type=F32)
        diag = jnp.sum(qh * kh, axis=-1, keepdims=True) * vh
        o = inter + intra + diag
        st_ref[hh] = st * e_incl[cs - 1:cs, sl] + lax.dot_general(vb, k_dec, tn, preferred_element_type=F32)
        gt = gt_ref[:, sl].astype(F32)
        on = _rms_scale(o, gn_ref[...]) * (gt * (1.0 / (1.0 + jnp.exp(-gt))))
        o_ref[:, sl] = on.astype(o_ref.dtype)


def _hgrn_recurrence(qig, fpre, lower_bound_params, layer_j, out_norm_gain, batch, seq, d_model):
    wb = HGRN_HEADS_PER_STEP * HEAD_DIM
    hb = d_model // wb
    tri, lvl, sgn = _hgrn_constants()
    n_layers = lower_bound_params.shape[0]

    def part(k):
        return pl.BlockSpec((None, HGRN_CHUNK, wb), lambda b, g, c: (b, c, k * hb + g))

    fixed = lambda b, g, c: (0, 0)
    out = pl.pallas_call(
        functools.partial(_hgrn_kernel, layer_j=layer_j),
        grid=(batch, hb, seq // HGRN_CHUNK),
        in_specs=[part(0), part(1), part(2), part(0),
                  pl.BlockSpec((n_layers, wb), lambda b, g, c: (0, g)),
                  pl.BlockSpec((1, HEAD_DIM), fixed),
                  pl.BlockSpec(tri.shape, fixed),
                  pl.BlockSpec(lvl.shape, fixed),
                  pl.BlockSpec(sgn.shape, fixed)],
        out_specs=pl.BlockSpec((None, HGRN_CHUNK, wb), lambda b, g, c: (b, c, g)),
        out_shape=jax.ShapeDtypeStruct((batch, seq, d_model), BF16),
        scratch_shapes=[pltpu.VMEM((HGRN_HEADS_PER_STEP, HEAD_DIM, HEAD_DIM), F32)],
        compiler_params=_params(("parallel", "parallel", "arbitrary")),
        name="hgrn_recurrence",
    )(qig, qig, qig, fpre, lower_bound_params, out_norm_gain.reshape(1, HEAD_DIM),
      jnp.asarray(tri, BF16), jnp.asarray(lvl), jnp.asarray(sgn))
    return out.reshape(batch * seq, d_model)


@jax.jit
def kernel(x, positions, norm_gains, w_att_in, w_att_out, w_rec_in, rec_lower_bounds, rec_out_norm,
           w_rec_out, w_ff1, w_ff2):
    batch, seq, d_model = x.shape
    depth = norm_gains.shape[0]
    tables = [_rope_tables(positions, dilation) for _, dilation in DILATED_GROUPS]
    w_ff1_bf16 = w_ff1.astype(BF16)
    h = x.reshape(batch * seq, d_model)
    for layer in range(depth):
        g = norm_gains[layer]
        j = layer // 2
        if layer % 2 == 0:
            outs, lses = [], []
            for gi, (window, dilation) in enumerate(DILATED_GROUPS):
                assert window // dilation == ATT_BLOCK
                qkv = _att_in_proj(h, g[0], w_att_in, j, gi, dilation, batch, seq, tables[gi])
                o, lse = _attention(qkv, dilation, batch, seq)
                outs.append(o)
                lses.append(lse)
            h = _merge_out_proj(h, g[1], w_att_out, j, outs, lses, batch, seq)
        else:
            per_part = d_model // PROJ_TILE_N
            qig = _in_proj(h, g[0], w_rec_in, j, lambda n: n + per_part * (n >= per_part), 3 * d_model, BF16)
            fpre = _in_proj(h, g[0], w_rec_in, j, lambda n: n + per_part, d_model, F32)
            mixed = _hgrn_recurrence(qig.reshape(batch, seq, 3 * d_model), fpre.reshape(batch, seq, d_model),
                                     rec_lower_bounds, j, rec_out_norm[j], batch, seq, d_model)
            h = _out_proj(h, g[1], w_rec_out, j, mixed)
        h = _mlp(h, g[2], g[3], w_ff1_bf16, w_ff2, layer)
    return h.reshape(batch, seq, d_model)
```

```python
import functools

import numpy as np
import jax
import jax.numpy as jnp
from jax import lax
from jax.experimental import pallas as pl
from jax.experimental.pallas import tpu as pltpu

F32 = jnp.float32
BF16 = jnp.bfloat16

NORM_EPS = 1e-6
MASK_VALUE = -1e30
LB_FLOOR = 1e-30

DILATED_GROUPS = ((128, 1), (512, 4), (2048, 16))
HEAD_DIM = 128
ATT_HEADS = 8
ATT_BLOCK = 128
ATT_GROUP_WIDTH = ATT_HEADS * HEAD_DIM
ROPE_THETA = 500000.0
ROPE_DIM = HEAD_DIM // 4
ROPE_HALF = ROPE_DIM // 2
TOKEN_TILE = 256
PROJ_TILE_M, PROJ_TILE_N = 2048, 512
PROJ_SUB = 512
OUT_TILE, OUT_SUB = 512, 256
MLP_TILE_M, MLP_TILE_F = 1024, 512
HGRN_CHUNK = 128
HGRN_LEVELS = 7
HGRN_HEADS_PER_STEP = 8
LOG2E = 1.4426950408889634

VMEM_LIMIT_BYTES = 56 * 1024 * 1024
SINGLE = pl.Buffered(1)


def _params(semantics):
    return pltpu.CompilerParams(dimension_semantics=semantics, vmem_limit_bytes=VMEM_LIMIT_BYTES)


def _rms_scale(x, gain):
    ms = jnp.mean(x * x, axis=-1, keepdims=True)
    return x * lax.rsqrt(ms + NORM_EPS) * gain


def _rope_kernel(pos_ref, invf_ref, cos_ref, sa_ref, sb_ref):
    ang = pos_ref[...].astype(F32) * invf_ref[...]
    lane = lax.broadcasted_iota(jnp.int32, ang.shape, 1)
    c = jnp.cos(ang)
    s = jnp.sin(ang)
    cos_ref[...] = jnp.where(lane < ROPE_DIM, c, 1.0)
    sa_ref[...] = jnp.where((lane >= ROPE_HALF) & (lane < ROPE_DIM), s, 0.0)
    sb_ref[...] = jnp.where(lane < ROPE_HALF, -s, 0.0)


def _rope_tables(positions, dilation):
    batch, seq = positions.shape
    t = batch * seq
    tr = 1024
    pos = positions.reshape(batch, seq // dilation, dilation).transpose(0, 2, 1)
    inv_freq = ROPE_THETA ** (-jnp.arange(ROPE_HALF, dtype=F32) / ROPE_HALF)
    invf = jnp.zeros((1, HEAD_DIM), F32).at[0, :ROPE_DIM].set(jnp.tile(inv_freq, 2))
    out = jax.ShapeDtypeStruct((t, HEAD_DIM), F32)
    tabs = pl.pallas_call(
        _rope_kernel,
        grid=(t // tr,),
        in_specs=[pl.BlockSpec((tr, 1), lambda i: (i, 0)),
                  pl.BlockSpec((1, HEAD_DIM), lambda i: (0, 0))],
        out_specs=[pl.BlockSpec((tr, HEAD_DIM), lambda i: (i, 0))] * 3,
        out_shape=[out, out, out],
        compiler_params=_params(("parallel",)),
        name="rope_tables",
    )(pos.reshape(t, 1), invf)
    return [tab.reshape(batch, dilation, seq // dilation, HEAD_DIM) for tab in tabs]


def _att_in_proj_kernel(*refs, dilation):
    refs = list(refs)
    planes_ref = refs.pop() if dilation > 1 else None
    x_ref, g_ref, w_ref, cos_ref, sa_ref, sb_ref, o_ref, wb_ref, xn_ref = refs
    tm, d_model = x_ref.shape
    n = tm // dilation

    @pl.when((pl.program_id(0) == 0) & (pl.program_id(1) == 0))
    def _():
        def cast_rows(i, carry):
            rows = pl.ds(pl.multiple_of(i * HEAD_DIM, HEAD_DIM), HEAD_DIM)
            wb_ref[rows, :] = w_ref[rows, :].astype(BF16)
            return carry
        lax.fori_loop(0, d_model // HEAD_DIM, cast_rows, 0)

    xn = _rms_scale(x_ref[...], g_ref[...])
    if dilation == 1:
        xn_ref[...] = xn.astype(BF16)
    else:
        for c in range(d_model // HEAD_DIM):
            planes_ref[c] = xn[:, c * HEAD_DIM:(c + 1) * HEAD_DIM]
        for r in range(dilation):
            for c in range(d_model // HEAD_DIM):
                rows = planes_ref[c, pl.ds(r, n, stride=dilation), :]
                xn_ref[r * n:(r + 1) * n, c * HEAD_DIM:(c + 1) * HEAD_DIM] = rows.astype(BF16)

    acc = jnp.dot(xn_ref[...], wb_ref[...], preferred_element_type=F32)

    def store(lo, val):
        for r in range(dilation):
            o_ref[r, :, lo:lo + val.shape[1]] = val[r * n:(r + 1) * n].astype(o_ref.dtype)

    cos, sa, sb = (ref[...].reshape(tm, HEAD_DIM) for ref in (cos_ref, sa_ref, sb_ref))
    scale = HEAD_DIM ** -0.5
    for part, mult in ((0, scale), (1, 1.0)):
        c, a, b = cos * mult, sa * mult, sb * mult
        for hd in range(ATT_HEADS):
            lo = part * ATT_GROUP_WIDTH + hd * HEAD_DIM
            xh = acc[:, lo:lo + HEAD_DIM]
            rot = (xh * c + pltpu.roll(xh, ROPE_HALF, 1) * a
                   + pltpu.roll(xh, HEAD_DIM - ROPE_HALF, 1) * b)
            store(lo, rot)
    store(2 * ATT_GROUP_WIDTH, acc[:, 2 * ATT_GROUP_WIDTH:])


def _att_in_proj(h, gain, w, layer, group, dilation, batch, seq, tables):
    t, d_model = h.shape
    tm = TOKEN_TILE
    tiles = seq // tm
    n = tm // dilation
    width = 3 * ATT_GROUP_WIDTH
    residue_tile = lambda b, i: (b, 0, i, 0)
    scratch = [pltpu.VMEM((d_model, width), BF16), pltpu.VMEM((tm, d_model), BF16)]
    if dilation > 1:
        scratch.append(pltpu.VMEM((d_model // HEAD_DIM, tm, HEAD_DIM), F32))
    return pl.pallas_call(
        functools.partial(_att_in_proj_kernel, dilation=dilation),
        grid=(batch, tiles),
        in_specs=[pl.BlockSpec((tm, d_model), lambda b, i: (b * tiles + i, 0)),
                  pl.BlockSpec((1, d_model), lambda b, i: (0, 0)),
                  pl.BlockSpec((None, d_model, width), lambda b, i: (layer, 0, group), pipeline_mode=SINGLE)]
                 + [pl.BlockSpec((None, dilation, n, HEAD_DIM), residue_tile)] * 3,
        out_specs=pl.BlockSpec((None, dilation, n, width), residue_tile),
        out_shape=jax.ShapeDtypeStruct((batch, dilation, seq // dilation, width), BF16),
        scratch_shapes=scratch,
        compiler_params=_params(("arbitrary", "arbitrary")),
        name="att_in_proj",
    )(h, gain.reshape(1, d_model), w, *tables)


def _in_proj_kernel(x_ref, g_ref, w_ref, o_ref, xn_ref):
    @pl.when(pl.program_id(1) == 0)
    def _():
        xn_ref[...] = _rms_scale(x_ref[...], g_ref[...]).astype(BF16)

    acc = jnp.dot(xn_ref[...], w_ref[...].astype(BF16), preferred_element_type=F32)
    o_ref[...] = acc.astype(o_ref.dtype)


def _in_proj(h, gain, w, layer, col_block, width, out_dtype):
    t, d_model = h.shape
    tm, tn = PROJ_TILE_M, PROJ_TILE_N
    return pl.pallas_call(
        _in_proj_kernel,
        grid=(t // tm, width // tn),
        in_specs=[pl.BlockSpec((tm, d_model), lambda i, j: (i, 0), pipeline_mode=SINGLE),
                  pl.BlockSpec((1, d_model), lambda i, j: (0, 0)),
                  pl.BlockSpec((None, d_model, tn), lambda i, j: (layer, 0, col_block(j)))],
        out_specs=pl.BlockSpec((tm, tn), lambda i, j: (i, j)),
        out_shape=jax.ShapeDtypeStruct((t, width), out_dtype),
        scratch_shapes=[pltpu.VMEM((tm, d_model), BF16)],
        compiler_params=_params(("parallel", "arbitrary")),
        name="in_proj",
    )(h, gain.reshape(1, d_model), w)


def _forget_proj_kernel(x_ref, g_ref, w_ref, lbp_ref, logf_ref, kk_ref, xn_ref, *, layer_j):
    @pl.when(pl.program_id(1) == 0)
    def _():
        xn_ref[...] = _rms_scale(x_ref[...], g_ref[...]).astype(BF16)

    lbp = lbp_ref[...]
    e = jnp.exp(lbp - jnp.max(lbp, axis=0, keepdims=True))
    p = e / jnp.sum(e, axis=0, keepdims=True)
    csum = p[0:1]
    for t in range(1, layer_j + 1):
        csum = csum + p[t:t + 1]
    lb = csum - p[0:1]
    lb_floor = jnp.maximum(lb, LB_FLOOR)
    one_minus_lb = 1.0 - lb

    wb = w_ref[...].astype(BF16)
    for lo in range(0, x_ref.shape[0], PROJ_SUB):
        rows = slice(lo, lo + PROJ_SUB)
        z = jnp.dot(xn_ref[rows, :], wb, preferred_element_type=F32)
        en = jnp.exp(-jnp.abs(z))
        rcp = 1.0 / (1.0 + en)
        pos = z >= 0.0
        logf_ref[rows, :] = jnp.log(lb_floor + one_minus_lb * (jnp.where(pos, 1.0, en) * rcp))
        kk_ref[rows, :] = (one_minus_lb * (jnp.where(pos, en, 1.0) * rcp) - (lb_floor - lb)).astype(kk_ref.dtype)


def _forget_proj(h, gain, w, layer, col_block, width, lower_bound_params):
    t, d_model = h.shape
    tm, tn = PROJ_TILE_M, PROJ_TILE_N
    n_layers = lower_bound_params.shape[0]
    out_spec = pl.BlockSpec((tm, tn), lambda i, j: (i, j))
    return pl.pallas_call(
        functools.partial(_forget_proj_kernel, layer_j=layer),
        grid=(t // tm, width // tn),
        in_specs=[pl.BlockSpec((tm, d_model), lambda i, j: (i, 0), pipeline_mode=SINGLE),
                  pl.BlockSpec((1, d_model), lambda i, j: (0, 0)),
                  pl.BlockSpec((None, d_model, tn), lambda i, j: (layer, 0, col_block(j))),
                  pl.BlockSpec((n_layers, tn), lambda i, j: (0, j))],
        out_specs=[out_spec, out_spec],
        out_shape=[jax.ShapeDtypeStruct((t, width), F32), jax.ShapeDtypeStruct((t, width), BF16)],
        scratch_shapes=[pltpu.VMEM((tm, d_model), BF16)],
        compiler_params=_params(("parallel", "arbitrary")),
        name="forget_proj",
    )(h, gain.reshape(1, d_model), w, lower_bound_params)


def _attention_kernel(*refs, has_prev):
    if has_prev:
        q_ref, kc_ref, vc_ref, o_ref, lse_ref, s_ref, prev_ref = refs
        kp_ref, vp_ref = prev_ref.at[0], prev_ref.at[1]
        first_block = pl.program_id(2) == 0

        @pl.when(first_block)
        def _():
            prev_ref[...] = jnp.zeros_like(prev_ref)
    else:
        q_ref, kc_ref, vc_ref, o_ref, lse_ref, s_ref = refs
    n_keys = s_ref.shape[2]
    row = lax.broadcasted_iota(jnp.int32, (ATT_BLOCK, n_keys), 0)
    col = lax.broadcasted_iota(jnp.int32, (ATT_BLOCK, n_keys), 1)
    if has_prev:
        mask = (col >= row) & (col <= row + ATT_BLOCK) & ((col >= ATT_BLOCK) | jnp.logical_not(first_block))
    else:
        mask = col <= row
    nt = (((1,), (1,)), ((), ()))
    heads = [slice(hd * HEAD_DIM, (hd + 1) * HEAD_DIM) for hd in range(ATT_HEADS)]

    def both(prev_ref, cur_ref, sl):
        if has_prev:
            return jnp.concatenate([prev_ref[:, sl], cur_ref[:, sl]], axis=0)
        return cur_ref[:, sl]

    maxes = []
    for hd, sl in enumerate(heads):
        s = lax.dot_general(q_ref[:, sl], both(kp_ref if has_prev else None, kc_ref, sl), nt,
                            preferred_element_type=F32)
        s = jnp.where(mask, s, MASK_VALUE)
        s_ref[hd] = s
        maxes.append(jnp.max(s, axis=-1, keepdims=True))
    lane = lax.broadcasted_iota(jnp.int32, (ATT_BLOCK, HEAD_DIM), 1)
    lse_tile = jnp.zeros((ATT_BLOCK, HEAD_DIM), F32)
    for hd, sl in enumerate(heads):
        p = jnp.exp(s_ref[hd] - maxes[hd])
        l = jnp.sum(p, axis=-1, keepdims=True)
        acc = jnp.dot(p.astype(BF16), both(vp_ref if has_prev else None, vc_ref, sl),
                      preferred_element_type=F32)
        o_ref[:, sl] = (acc / l).astype(o_ref.dtype)
        lse_tile = jnp.where(lane == hd, maxes[hd] + jnp.log(l), lse_tile)
    lse_ref[...] = lse_tile
    if has_prev:
        kp_ref[...] = kc_ref[...]
        vp_ref[...] = vc_ref[...]


def _attention(qkv, dilation, batch, seq):
    length = seq // dilation
    nb = length // ATT_BLOCK
    has_prev = nb > 1
    blk_shape = (None, None, ATT_BLOCK, ATT_GROUP_WIDTH)

    def cur(part):
        return pl.BlockSpec(blk_shape, lambda b, r, i: (b, r, i, part))

    in_specs = [cur(0), cur(1), cur(2)]
    n_keys = 2 * ATT_BLOCK if has_prev else ATT_BLOCK
    scratch = [pltpu.VMEM((ATT_HEADS, ATT_BLOCK, n_keys), F32)]
    if has_prev:
        scratch.append(pltpu.VMEM((2, ATT_BLOCK, ATT_GROUP_WIDTH), BF16))
    return pl.pallas_call(
        functools.partial(_attention_kernel, has_prev=has_prev),
        grid=(batch, dilation, nb),
        in_specs=in_specs,
        out_specs=[pl.BlockSpec(blk_shape, lambda b, r, i: (b, r, i, 0)),
                   pl.BlockSpec((None, None, ATT_BLOCK, HEAD_DIM), lambda b, r, i: (b, r, i, 0))],
        out_shape=[jax.ShapeDtypeStruct((batch, dilation, length, ATT_GROUP_WIDTH), BF16),
                   jax.ShapeDtypeStruct((batch, dilation, length, HEAD_DIM), F32)],
        scratch_shapes=scratch,
        compiler_params=_params(("parallel", "parallel", "arbitrary")),
        name="banded_attention",
    )(*([qkv] * len(in_specs)))


def _cast_weight_once(w_ref, wb_ref, first):
    @pl.when(first)
    def _():
        wb_ref[...] = w_ref[...].astype(BF16)


def _project_norm_residual(a_ref, wb_ref, h_ref, g_ref, out_ref):
    for lo in range(0, a_ref.shape[0], OUT_SUB):
        rows = slice(lo, lo + OUT_SUB)
        y = jnp.dot(a_ref[rows, :], wb_ref[...], preferred_element_type=F32)
        out_ref[rows, :] = h_ref[rows, :] + _rms_scale(y, g_ref[...])


def _merge_out_proj_kernel(o1, o2, o3, l1, l2, l3, h_ref, g_ref, w_ref, out_ref, wb_ref, ot_ref, lt_ref, a_ref):
    _cast_weight_once(w_ref, wb_ref, (pl.program_id(0) == 0) & (pl.program_id(1) == 0))
    tm = h_ref.shape[0]
    for gi, (o_g, l_g) in enumerate(((o2, l2), (o3, l3))):
        dilation = o_g.shape[0]
        n = tm // dilation
        for r in range(dilation):
            rows = pl.ds(r, n, stride=dilation)
            lt_ref[gi, rows, :] = l_g[r]
            for hd in range(ATT_HEADS):
                ot_ref[gi, hd, rows, :] = o_g[r, :, hd * HEAD_DIM:(hd + 1) * HEAD_DIM].astype(F32)
    lses = (l1[0], lt_ref[0], lt_ref[1])
    m = jnp.maximum(jnp.maximum(lses[0], lses[1]), lses[2])
    es = [jnp.exp(v - m) for v in lses]
    den = es[0] + es[1] + es[2]
    ws = [e / den for e in es]
    for hd in range(ATT_HEADS):
        sl = slice(hd * HEAD_DIM, (hd + 1) * HEAD_DIM)
        acc = ws[0][:, hd:hd + 1] * o1[0, :, sl].astype(F32)
        acc = acc + ws[1][:, hd:hd + 1] * ot_ref[0, hd]
        acc = acc + ws[2][:, hd:hd + 1] * ot_ref[1, hd]
        a_ref[:, sl] = acc.astype(BF16)
    _project_norm_residual(a_ref, wb_ref, h_ref, g_ref, out_ref)


def _merge_out_proj(h, gain, w, layer, outs, lses, batch, seq):
    t, d_model = h.shape
    k = w.shape[1]
    tm = OUT_TILE
    tiles = seq // tm
    token_tile = lambda b, i: (b * tiles + i, 0)
    fixed = lambda b, i: (0, 0)

    def group_spec(arr):
        dilation, width = arr.shape[1], arr.shape[3]
        return pl.BlockSpec((None, dilation, tm // dilation, width), lambda b, i: (b, 0, i, 0))

    in_specs = [group_spec(a) for a in outs] + [group_spec(a) for a in lses]
    in_specs += [pl.BlockSpec((tm, d_model), token_tile), pl.BlockSpec((1, d_model), fixed),
                 pl.BlockSpec((None, k, d_model), lambda b, i: (layer, 0, 0), pipeline_mode=SINGLE)]
    return pl.pallas_call(
        _merge_out_proj_kernel,
        grid=(batch, tiles),
        in_specs=in_specs,
        out_specs=pl.BlockSpec((tm, d_model), token_tile),
        out_shape=jax.ShapeDtypeStruct((t, d_model), F32),
        scratch_shapes=[pltpu.VMEM((k, d_model), BF16),
                        pltpu.VMEM((2, ATT_HEADS, tm, HEAD_DIM), F32), pltpu.VMEM((2, tm, HEAD_DIM), F32),
                        pltpu.VMEM((tm, k), BF16)],
        compiler_params=_params(("arbitrary", "arbitrary")),
        name="merge_out_proj",
    )(*outs, *lses, h, gain.reshape(1, d_model), w)


def _out_proj_kernel(a_ref, h_ref, g_ref, w_ref, out_ref, wb_ref):
    _cast_weight_once(w_ref, wb_ref, pl.program_id(0) == 0)
    _project_norm_residual(a_ref, wb_ref, h_ref, g_ref, out_ref)


def _out_proj(h, gain, w, layer, mixed):
    t, d_model = h.shape
    k = w.shape[1]
    tm = OUT_TILE
    row = lambda i: (i, 0)
    fixed = lambda i: (0, 0)
    return pl.pallas_call(
        _out_proj_kernel,
        grid=(t // tm,),
        in_specs=[pl.BlockSpec((tm, k), row), pl.BlockSpec((tm, d_model), row),
                  pl.BlockSpec((1, d_model), fixed),
                  pl.BlockSpec((None, k, d_model), lambda i: (layer, 0, 0), pipeline_mode=SINGLE)],
        out_specs=pl.BlockSpec((tm, d_model), row),
        out_shape=jax.ShapeDtypeStruct((t, d_model), F32),
        scratch_shapes=[pltpu.VMEM((k, d_model), BF16)],
        compiler_params=_params(("arbitrary",)),
        name="out_proj",
    )(mixed, h, gain.reshape(1, d_model), w)


def _mlp_kernel(h_ref, g_in_ref, g_out_ref, w1_ref, w2_ref, out_ref, un_ref):
    f = pl.program_id(1)

    @pl.when(f == 0)
    def _():
        un_ref[...] = _rms_scale(h_ref[...], g_in_ref[...]).astype(BF16)
        out_ref[...] = jnp.zeros_like(out_ref)

    a = jnp.dot(un_ref[...], w1_ref[...].astype(BF16), preferred_element_type=F32)
    a = jnp.square(jnp.maximum(a, 0.0))
    out_ref[...] += jnp.dot(a.astype(BF16), w2_ref[...].astype(BF16), preferred_element_type=F32)

    @pl.when(f == pl.num_programs(1) - 1)
    def _():
        out_ref[...] = h_ref[...] + _rms_scale(out_ref[...], g_out_ref[...])


def _mlp(h, g_in, g_out, w1, w2, layer):
    t, d_model = h.shape
    d_ff = w1.shape[2]
    tm, tf = MLP_TILE_M, MLP_TILE_F
    return pl.pallas_call(
        _mlp_kernel,
        grid=(t // tm, d_ff // tf),
        in_specs=[pl.BlockSpec((tm, d_model), lambda m, f: (m, 0), pipeline_mode=SINGLE),
                  pl.BlockSpec((1, d_model), lambda m, f: (0, 0)),
                  pl.BlockSpec((1, d_model), lambda m, f: (0, 0)),
                  pl.BlockSpec((None, d_model, tf), lambda m, f: (layer, 0, f)),
                  pl.BlockSpec((None, tf, d_model), lambda m, f: (layer, f, 0))],
        out_specs=pl.BlockSpec((tm, d_model), lambda m, f: (m, 0)),
        out_shape=jax.ShapeDtypeStruct((t, d_model), F32),
        scratch_shapes=[pltpu.VMEM((tm, d_model), BF16)],
        compiler_params=_params(("parallel", "arbitrary")),
        name="mlp",
    )(h, g_in.reshape(1, d_model), g_out.reshape(1, d_model), w1, w2)


def _hgrn_constants():
    r = np.arange(HGRN_CHUNK)[:, None]
    c = np.arange(HGRN_CHUNK)[None, :]
    tri = (c <= r).astype(np.float32)
    msb = np.floor(np.log2(np.maximum(r ^ c, 1))).astype(np.int32)
    lvl = np.where(c < r, msb, -1).astype(np.int32)
    sgn = np.concatenate([np.where((r >> j) & 1 == 1, LOG2E, -LOG2E) * np.ones_like(c)
                          for j in range(2, HGRN_LEVELS)], axis=0).astype(np.float32)
    return tri, lvl, sgn


def _hgrn_kernel(q_ref, i_ref, gt_ref, logf_ref, kk_ref, gn_ref, tri_ref, lvl_ref, sgn_ref, o_ref, st_ref):
    nh = HGRN_HEADS_PER_STEP
    cs = HGRN_CHUNK
    width = nh * HEAD_DIM

    @pl.when(pl.program_id(2) == 0)
    def _():
        st_ref[...] = jnp.zeros_like(st_ref)

    log_f = logf_ref[...]

    g_hi = log_f.astype(BF16)
    r1 = log_f - g_hi.astype(F32)
    g_mid = r1.astype(BF16)
    g_lo = (r1 - g_mid.astype(F32)).astype(BF16)
    tri = tri_ref[...]
    b = (jnp.dot(tri, g_hi, preferred_element_type=F32)
         + jnp.dot(tri, g_mid, preferred_element_type=F32)
         + jnp.dot(tri, g_lo, preferred_element_type=F32))
    b_last = b[cs - 1:cs, :]

    row = lax.broadcasted_iota(jnp.int32, (cs, width), 0)
    up = pltpu.roll(log_f, cs - 1, 0)
    down = pltpu.roll(log_f, 1, 0)
    r4 = row & 3
    level_e = [
        jnp.exp(jnp.where((row & 1) == 1, log_f, 0.0)),
        jnp.exp(jnp.where(r4 == 0, up, jnp.where(r4 == 1, 0.0, jnp.where(r4 == 2, log_f, log_f + down)))),
    ]
    for j in range(2, HGRN_LEVELS):
        half = 1 << j
        nblk = cs // (2 * half)
        b3 = b.reshape(nblk, 2 * half, width)
        mid = jnp.broadcast_to(b3[:, half - 1:half, :], b3.shape).reshape(cs, width)
        sign_log2e = jnp.tile(sgn_ref[(j - 2) * cs:(j - 1) * cs, :], (1, nh))
        level_e.append(jnp.exp2((b - mid) * sign_log2e))
    e_incl = jnp.exp(b)
    e_suffix = jnp.exp(b_last - b)

    lvl = lvl_ref[...]
    level_mask = [lvl == j for j in range(HGRN_LEVELS)]
    nt = (((1,), (1,)), ((), ()))
    tn = (((0,), (0,)), ((), ()))
    for hh in range(nh):
        sl = slice(hh * HEAD_DIM, (hh + 1) * HEAD_DIM)
        qb = q_ref[:, sl]
        qh = qb.astype(F32)
        kb = kk_ref[:, sl]
        kh = kb.astype(F32)
        vb = i_ref[:, sl]
        vh = vb.astype(F32)
        a_mat = jnp.zeros((cs, cs), F32)
        for j in range(HGRN_LEVELS):
            ej = level_e[j][:, sl].astype(BF16)
            aj = lax.dot_general(qb * ej, kb * ej, nt, preferred_element_type=F32)
            a_mat = jnp.where(level_mask[j], aj, a_mat)
        q_dec = qb * e_incl[:, sl].astype(BF16)
        k_dec = kb * e_suffix[:, sl].astype(BF16)
        st = st_ref[hh]
        inter = lax.dot_general(q_dec, st.astype(BF16), nt, preferred_element_type=F32)
        intra = jnp.dot(a_mat.astype(BF16), vb, preferred_element_type=F32)
        diag = jnp.sum(qh * kh, axis=-1, keepdims=True) * vh
        o = inter + intra + diag
        st_ref[hh] = st * e_incl[cs - 1:cs, sl] + lax.dot_general(vb, k_dec, tn, preferred_element_type=F32)
        gt = gt_ref[:, sl].astype(F32)
        on = _rms_scale(o, gn_ref[...]) * (gt * (1.0 / (1.0 + jnp.exp(-gt))))
        o_ref[:, sl] = on.astype(o_ref.dtype)


def _hgrn_recurrence(qig, log_f, kk, out_norm_gain, batch, seq, d_model):
    wb = HGRN_HEADS_PER_STEP * HEAD_DIM
    hb = d_model // wb
    tri, lvl, sgn = _hgrn_constants()

    def part(k):
        return pl.BlockSpec((None, HGRN_CHUNK, wb), lambda b, g, c: (b, c, k * hb + g))

    fixed = lambda b, g, c: (0, 0)
    out = pl.pallas_call(
        _hgrn_kernel,
        grid=(batch, hb, seq // HGRN_CHUNK),
        in_specs=[part(0), part(1), part(2), part(0), part(0),
                  pl.BlockSpec((1, HEAD_DIM), fixed),
                  pl.BlockSpec(tri.shape, fixed),
                  pl.BlockSpec(lvl.shape, fixed),
                  pl.BlockSpec(sgn.shape, fixed)],
        out_specs=pl.BlockSpec((None, HGRN_CHUNK, wb), lambda b, g, c: (b, c, g)),
        out_shape=jax.ShapeDtypeStruct((batch, seq, d_model), BF16),
        scratch_shapes=[pltpu.VMEM((HGRN_HEADS_PER_STEP, HEAD_DIM, HEAD_DIM), F32)],
        compiler_params=_params(("parallel", "parallel", "arbitrary")),
        name="hgrn_recurrence",
    )(qig, qig, qig, log_f, kk, out_norm_gain.reshape(1, HEAD_DIM),
      jnp.asarray(tri, BF16), jnp.asarray(lvl), jnp.asarray(sgn))
    return out.reshape(batch * seq, d_model)


@jax.jit
def kernel(x, positions, norm_gains, w_att_in, w_att_out, w_rec_in, rec_lower_bounds, rec_out_norm,
           w_rec_out, w_ff1, w_ff2):
    batch, seq, d_model = x.shape
    depth = norm_gains.shape[0]
    tables = [_rope_tables(positions, dilation) for _, dilation in DILATED_GROUPS]
    h = x.reshape(batch * seq, d_model)
    for layer in range(depth):
        g = norm_gains[layer]
        j = layer // 2
        if layer % 2 == 0:
            outs, lses = [], []
            for gi, (window, dilation) in enumerate(DILATED_GROUPS):
                assert window // dilation == ATT_BLOCK
                qkv = _att_in_proj(h, g[0], w_att_in, j, gi, dilation, batch, seq, tables[gi])
                o, lse = _attention(qkv, dilation, batch, seq)
                outs.append(o)
                lses.append(lse)
            h = _merge_out_proj(h, g[1], w_att_out, j, outs, lses, batch, seq)
        else:
            per_part = d_model // PROJ_TILE_N
            qig = _in_proj(h, g[0], w_rec_in, j, lambda n: n + per_part * (n >= per_part), 3 * d_model, BF16)
            log_f, kk = _forget_proj(h, g[0], w_rec_in, j, lambda n: n + per_part, d_model, rec_lower_bounds)
            mixed = _hgrn_recurrence(qig.reshape(batch, seq, 3 * d_model), log_f.reshape(batch, seq, d_model),
                                     kk.reshape(batch, seq, d_model), rec_out_norm[j], batch, seq, d_model)
            h = _out_proj(h, g[1], w_rec_out, j, mixed)
        h = _mlp(h, g[2], g[3], w_ff1, w_ff2, layer)
    return h.reshape(batch, seq, d_model)
```

```python
import functools

import numpy as np
import jax
import jax.numpy as jnp
from jax import lax
from jax.experimental import pallas as pl
from jax.experimental.pallas import tpu as pltpu

F32 = jnp.float32
BF16 = jnp.bfloat16

NORM_EPS = 1e-6
MASK_VALUE = -1e30
LB_FLOOR = 1e-30

DILATED_GROUPS = ((128, 1), (512, 4), (2048, 16))
HEAD_DIM = 128
ATT_HEADS = 8
ATT_BLOCK = 128
ATT_GROUP_WIDTH = ATT_HEADS * HEAD_DIM
ROPE_THETA = 500000.0
ROPE_DIM = HEAD_DIM // 4
ROPE_HALF = ROPE_DIM // 2
TOKEN_TILE = 256
PROJ_TILE_M, PROJ_SUB = 512, 256
OUT_TILE, OUT_SUB = 512, 256
MLP_TILE_M, MLP_TILE_F = 1024, 512
HGRN_CHUNK = 128
HGRN_LEVELS = 7
HGRN_HEADS_PER_STEP = 8
LOG2E = 1.4426950408889634

VMEM_LIMIT_BYTES = 56 * 1024 * 1024
SINGLE = pl.Buffered(1)


def _params(semantics):
    return pltpu.CompilerParams(dimension_semantics=semantics, vmem_limit_bytes=VMEM_LIMIT_BYTES)


def _rms_scale(x, gain):
    ms = jnp.mean(x * x, axis=-1, keepdims=True)
    return x * lax.rsqrt(ms + NORM_EPS) * gain


def _rope_kernel(pos_ref, invf_ref, cos_ref, sa_ref, sb_ref):
    ang = pos_ref[...].astype(F32) * invf_ref[...]
    lane = lax.broadcasted_iota(jnp.int32, ang.shape, 1)
    c = jnp.cos(ang)
    s = jnp.sin(ang)
    cos_ref[...] = jnp.where(lane < ROPE_DIM, c, 1.0)
    sa_ref[...] = jnp.where((lane >= ROPE_HALF) & (lane < ROPE_DIM), s, 0.0)
    sb_ref[...] = jnp.where(lane < ROPE_HALF, -s, 0.0)


def _rope_tables(positions, dilation):
    batch, seq = positions.shape
    t = batch * seq
    tr = 1024
    pos = positions.reshape(batch, seq // dilation, dilation).transpose(0, 2, 1)
    inv_freq = ROPE_THETA ** (-jnp.arange(ROPE_HALF, dtype=F32) / ROPE_HALF)
    invf = jnp.zeros((1, HEAD_DIM), F32).at[0, :ROPE_DIM].set(jnp.tile(inv_freq, 2))
    out = jax.ShapeDtypeStruct((t, HEAD_DIM), F32)
    tabs = pl.pallas_call(
        _rope_kernel,
        grid=(t // tr,),
        in_specs=[pl.BlockSpec((tr, 1), lambda i: (i, 0)),
                  pl.BlockSpec((1, HEAD_DIM), lambda i: (0, 0))],
        out_specs=[pl.BlockSpec((tr, HEAD_DIM), lambda i: (i, 0))] * 3,
        out_shape=[out, out, out],
        compiler_params=_params(("parallel",)),
        name="rope_tables",
    )(pos.reshape(t, 1), invf)
    return [tab.reshape(batch, dilation, seq // dilation, HEAD_DIM) for tab in tabs]


def _att_in_proj_kernel(*refs, dilation):
    refs = list(refs)
    planes_ref = refs.pop() if dilation > 1 else None
    x_ref, g_ref, w_ref, cos_ref, sa_ref, sb_ref, o_ref, wb_ref, xn_ref = refs
    tm, d_model = x_ref.shape
    n = tm // dilation

    _cast_rows_once(w_ref, wb_ref, (pl.program_id(0) == 0) & (pl.program_id(1) == 0))
    xn = _rms_scale(x_ref[...], g_ref[...])
    if dilation == 1:
        xn_ref[...] = xn.astype(BF16)
    else:
        for c in range(d_model // HEAD_DIM):
            planes_ref[c] = xn[:, c * HEAD_DIM:(c + 1) * HEAD_DIM]
        for r in range(dilation):
            for c in range(d_model // HEAD_DIM):
                rows = planes_ref[c, pl.ds(r, n, stride=dilation), :]
                xn_ref[r * n:(r + 1) * n, c * HEAD_DIM:(c + 1) * HEAD_DIM] = rows.astype(BF16)

    acc = jnp.dot(xn_ref[...], wb_ref[...], preferred_element_type=F32)

    def store(lo, val):
        for r in range(dilation):
            o_ref[r, :, lo:lo + val.shape[1]] = val[r * n:(r + 1) * n].astype(o_ref.dtype)

    cos, sa, sb = (ref[...].reshape(tm, HEAD_DIM) for ref in (cos_ref, sa_ref, sb_ref))
    scale = HEAD_DIM ** -0.5
    for part, mult in ((0, scale), (1, 1.0)):
        c, a, b = cos * mult, sa * mult, sb * mult
        for hd in range(ATT_HEADS):
            lo = part * ATT_GROUP_WIDTH + hd * HEAD_DIM
            xh = acc[:, lo:lo + HEAD_DIM]
            rot = (xh * c + pltpu.roll(xh, ROPE_HALF, 1) * a
                   + pltpu.roll(xh, HEAD_DIM - ROPE_HALF, 1) * b)
            store(lo, rot)
    store(2 * ATT_GROUP_WIDTH, acc[:, 2 * ATT_GROUP_WIDTH:])


def _att_in_proj(h, gain, w, layer, group, dilation, batch, seq, tables):
    t, d_model = h.shape
    tm = TOKEN_TILE
    tiles = seq // tm
    n = tm // dilation
    width = 3 * ATT_GROUP_WIDTH
    residue_tile = lambda b, i: (b, 0, i, 0)
    scratch = [pltpu.VMEM((d_model, width), BF16), pltpu.VMEM((tm, d_model), BF16)]
    if dilation > 1:
        scratch.append(pltpu.VMEM((d_model // HEAD_DIM, tm, HEAD_DIM), F32))
    return pl.pallas_call(
        functools.partial(_att_in_proj_kernel, dilation=dilation),
        grid=(batch, tiles),
        in_specs=[pl.BlockSpec((tm, d_model), lambda b, i: (b * tiles + i, 0)),
                  pl.BlockSpec((1, d_model), lambda b, i: (0, 0)),
                  pl.BlockSpec((None, d_model, width), lambda b, i: (layer, 0, group), pipeline_mode=SINGLE)]
                 + [pl.BlockSpec((None, dilation, n, HEAD_DIM), residue_tile)] * 3,
        out_specs=pl.BlockSpec((None, dilation, n, width), residue_tile),
        out_shape=jax.ShapeDtypeStruct((batch, dilation, seq // dilation, width), BF16),
        scratch_shapes=scratch,
        compiler_params=_params(("arbitrary", "arbitrary")),
        name="att_in_proj",
    )(h, gain.reshape(1, d_model), w, *tables)


def _cast_rows_once(w_ref, wb_ref, first):
    @pl.when(first)
    def _():
        def cast_rows(i, carry):
            rows = pl.ds(pl.multiple_of(i * HEAD_DIM, HEAD_DIM), HEAD_DIM)
            wb_ref[rows, :] = w_ref[rows, :].astype(BF16)
            return carry
        lax.fori_loop(0, w_ref.shape[0] // HEAD_DIM, cast_rows, 0)


def _part_proj_kernel(x_ref, g_ref, w_ref, o_ref, wb_ref):
    _cast_rows_once(w_ref, wb_ref, pl.program_id(1) == 0)
    for lo in range(0, x_ref.shape[0], PROJ_SUB):
        rows = slice(lo, lo + PROJ_SUB)
        xn = _rms_scale(x_ref[rows, :], g_ref[...]).astype(BF16)
        o_ref[rows, :] = jnp.dot(xn, wb_ref[...], preferred_element_type=F32).astype(o_ref.dtype)


def _forget_proj_kernel(x_ref, g_ref, w_ref, lbp_ref, logf_ref, kk_ref, wb_ref, *, layer_j):
    _cast_rows_once(w_ref, wb_ref, pl.program_id(0) == 0)
    lbp = lbp_ref[...]
    e = jnp.exp(lbp - jnp.max(lbp, axis=0, keepdims=True))
    p = e / jnp.sum(e, axis=0, keepdims=True)
    csum = p[0:1]
    for t in range(1, layer_j + 1):
        csum = csum + p[t:t + 1]
    lb = csum - p[0:1]
    lb_floor = jnp.maximum(lb, LB_FLOOR)
    one_minus_lb = 1.0 - lb
    for lo in range(0, x_ref.shape[0], PROJ_SUB):
        rows = slice(lo, lo + PROJ_SUB)
        xn = _rms_scale(x_ref[rows, :], g_ref[...]).astype(BF16)
        z = jnp.dot(xn, wb_ref[...], preferred_element_type=F32)
        en = jnp.exp(-jnp.abs(z))
        rcp = 1.0 / (1.0 + en)
        pos = z >= 0.0
        logf_ref[rows, :] = jnp.log(lb_floor + one_minus_lb * (jnp.where(pos, 1.0, en) * rcp))
        kk_ref[rows, :] = (one_minus_lb * (jnp.where(pos, en, 1.0) * rcp) - (lb_floor - lb)).astype(kk_ref.dtype)


def _part_proj(h, gain, w, layer, parts):
    t, d_model = h.shape
    tm = PROJ_TILE_M

    def part_of(p):
        idx = parts[-1]
        for k in range(len(parts) - 2, -1, -1):
            idx = jnp.where(p == k, parts[k], idx)
        return idx

    return pl.pallas_call(
        _part_proj_kernel,
        grid=(len(parts), t // tm),
        in_specs=[pl.BlockSpec((tm, d_model), lambda p, i: (i, 0)),
                  pl.BlockSpec((1, d_model), lambda p, i: (0, 0)),
                  pl.BlockSpec((None, d_model, d_model), lambda p, i: (layer, 0, part_of(p)),
                               pipeline_mode=SINGLE)],
        out_specs=pl.BlockSpec((None, tm, d_model), lambda p, i: (p, i, 0)),
        out_shape=jax.ShapeDtypeStruct((len(parts), t, d_model), BF16),
        scratch_shapes=[pltpu.VMEM((d_model, d_model), BF16)],
        compiler_params=_params(("arbitrary", "arbitrary")),
        name="part_proj",
    )(h, gain.reshape(1, d_model), w)


def _forget_proj(h, gain, w, layer, part, lower_bound_params):
    t, d_model = h.shape
    tm = PROJ_TILE_M
    n_layers = lower_bound_params.shape[0]
    row = lambda i: (i, 0)
    fixed = lambda i: (0, 0)
    return pl.pallas_call(
        functools.partial(_forget_proj_kernel, layer_j=layer),
        grid=(t // tm,),
        in_specs=[pl.BlockSpec((tm, d_model), row),
                  pl.BlockSpec((1, d_model), fixed),
                  pl.BlockSpec((None, d_model, d_model), lambda i: (layer, 0, part), pipeline_mode=SINGLE),
                  pl.BlockSpec((n_layers, d_model), fixed)],
        out_specs=[pl.BlockSpec((tm, d_model), row), pl.BlockSpec((tm, d_model), row)],
        out_shape=[jax.ShapeDtypeStruct((t, d_model), F32), jax.ShapeDtypeStruct((t, d_model), BF16)],
        scratch_shapes=[pltpu.VMEM((d_model, d_model), BF16)],
        compiler_params=_params(("arbitrary",)),
        name="forget_proj",
    )(h, gain.reshape(1, d_model), w, lower_bound_params)


def _attention_kernel(*refs, has_prev):
    if has_prev:
        q_ref, kc_ref, vc_ref, o_ref, lse_ref, s_ref, prev_ref = refs
        kp_ref, vp_ref = prev_ref.at[0], prev_ref.at[1]
        first_block = pl.program_id(2) == 0

        @pl.when(first_block)
        def _():
            prev_ref[...] = jnp.zeros_like(prev_ref)
    else:
        q_ref, kc_ref, vc_ref, o_ref, lse_ref, s_ref = refs
    n_keys = s_ref.shape[2]
    row = lax.broadcasted_iota(jnp.int32, (ATT_BLOCK, n_keys), 0)
    col = lax.broadcasted_iota(jnp.int32, (ATT_BLOCK, n_keys), 1)
    if has_prev:
        mask = (col >= row) & (col <= row + ATT_BLOCK) & ((col >= ATT_BLOCK) | jnp.logical_not(first_block))
    else:
        mask = col <= row
    nt = (((1,), (1,)), ((), ()))
    heads = [slice(hd * HEAD_DIM, (hd + 1) * HEAD_DIM) for hd in range(ATT_HEADS)]

    def both(prev_ref, cur_ref, sl):
        if has_prev:
            return jnp.concatenate([prev_ref[:, sl], cur_ref[:, sl]], axis=0)
        return cur_ref[:, sl]

    maxes = []
    for hd, sl in enumerate(heads):
        s = lax.dot_general(q_ref[:, sl], both(kp_ref if has_prev else None, kc_ref, sl), nt,
                            preferred_element_type=F32)
        s = jnp.where(mask, s, MASK_VALUE)
        s_ref[hd] = s
        maxes.append(jnp.max(s, axis=-1, keepdims=True))
    lane = lax.broadcasted_iota(jnp.int32, (ATT_BLOCK, HEAD_DIM), 1)
    lse_tile = jnp.zeros((ATT_BLOCK, HEAD_DIM), F32)
    for hd, sl in enumerate(heads):
        p = jnp.exp(s_ref[hd] - maxes[hd])
        l = jnp.sum(p, axis=-1, keepdims=True)
        acc = jnp.dot(p.astype(BF16), both(vp_ref if has_prev else None, vc_ref, sl),
                      preferred_element_type=F32)
        o_ref[:, sl] = (acc / l).astype(o_ref.dtype)
        lse_tile = jnp.where(lane == hd, maxes[hd] + jnp.log(l), lse_tile)
    lse_ref[...] = lse_tile
    if has_prev:
        kp_ref[...] = kc_ref[...]
        vp_ref[...] = vc_ref[...]


def _attention(qkv, dilation, batch, seq):
    length = seq // dilation
    nb = length // ATT_BLOCK
    has_prev = nb > 1
    blk_shape = (None, None, ATT_BLOCK, ATT_GROUP_WIDTH)

    def cur(part):
        return pl.BlockSpec(blk_shape, lambda b, r, i: (b, r, i, part))

    in_specs = [cur(0), cur(1), cur(2)]
    n_keys = 2 * ATT_BLOCK if has_prev else ATT_BLOCK
    scratch = [pltpu.VMEM((ATT_HEADS, ATT_BLOCK, n_keys), F32)]
    if has_prev:
        scratch.append(pltpu.VMEM((2, ATT_BLOCK, ATT_GROUP_WIDTH), BF16))
    return pl.pallas_call(
        functools.partial(_attention_kernel, has_prev=has_prev),
        grid=(batch, dilation, nb),
        in_specs=in_specs,
        out_specs=[pl.BlockSpec(blk_shape, lambda b, r, i: (b, r, i, 0)),
                   pl.BlockSpec((None, None, ATT_BLOCK, HEAD_DIM), lambda b, r, i: (b, r, i, 0))],
        out_shape=[jax.ShapeDtypeStruct((batch, dilation, length, ATT_GROUP_WIDTH), BF16),
                   jax.ShapeDtypeStruct((batch, dilation, length, HEAD_DIM), F32)],
        scratch_shapes=scratch,
        compiler_params=_params(("parallel", "parallel", "arbitrary")),
        name="banded_attention",
    )(*([qkv] * len(in_specs)))


def _cast_weight_once(w_ref, wb_ref, first):
    @pl.when(first)
    def _():
        wb_ref[...] = w_ref[...].astype(BF16)


def _project_norm_residual(a_ref, wb_ref, h_ref, g_ref, out_ref):
    for lo in range(0, a_ref.shape[0], OUT_SUB):
        rows = slice(lo, lo + OUT_SUB)
        y = jnp.dot(a_ref[rows, :], wb_ref[...], preferred_element_type=F32)
        out_ref[rows, :] = h_ref[rows, :] + _rms_scale(y, g_ref[...])


def _merge_out_proj_kernel(o1, o2, o3, l1, l2, l3, h_ref, g_ref, w_ref, out_ref, wb_ref, ot_ref, lt_ref, a_ref):
    _cast_weight_once(w_ref, wb_ref, (pl.program_id(0) == 0) & (pl.program_id(1) == 0))
    tm = h_ref.shape[0]
    for gi, (o_g, l_g) in enumerate(((o2, l2), (o3, l3))):
        dilation = o_g.shape[0]
        n = tm // dilation
        for r in range(dilation):
            rows = pl.ds(r, n, stride=dilation)
            lt_ref[gi, rows, :] = l_g[r]
            for hd in range(ATT_HEADS):
                ot_ref[gi, hd, rows, :] = o_g[r, :, hd * HEAD_DIM:(hd + 1) * HEAD_DIM].astype(F32)
    lses = (l1[0], lt_ref[0], lt_ref[1])
    m = jnp.maximum(jnp.maximum(lses[0], lses[1]), lses[2])
    es = [jnp.exp(v - m) for v in lses]
    den = es[0] + es[1] + es[2]
    ws = [e / den for e in es]
    for hd in range(ATT_HEADS):
        sl = slice(hd * HEAD_DIM, (hd + 1) * HEAD_DIM)
        acc = ws[0][:, hd:hd + 1] * o1[0, :, sl].astype(F32)
        acc = acc + ws[1][:, hd:hd + 1] * ot_ref[0, hd]
        acc = acc + ws[2][:, hd:hd + 1] * ot_ref[1, hd]
        a_ref[:, sl] = acc.astype(BF16)
    _project_norm_residual(a_ref, wb_ref, h_ref, g_ref, out_ref)


def _merge_out_proj(h, gain, w, layer, outs, lses, batch, seq):
    t, d_model = h.shape
    k = w.shape[1]
    tm = OUT_TILE
    tiles = seq // tm
    token_tile = lambda b, i: (b * tiles + i, 0)
    fixed = lambda b, i: (0, 0)

    def group_spec(arr):
        dilation, width = arr.shape[1], arr.shape[3]
        return pl.BlockSpec((None, dilation, tm // dilation, width), lambda b, i: (b, 0, i, 0))

    in_specs = [group_spec(a) for a in outs] + [group_spec(a) for a in lses]
    in_specs += [pl.BlockSpec((tm, d_model), token_tile), pl.BlockSpec((1, d_model), fixed),
                 pl.BlockSpec((None, k, d_model), lambda b, i: (layer, 0, 0), pipeline_mode=SINGLE)]
    return pl.pallas_call(
        _merge_out_proj_kernel,
        grid=(batch, tiles),
        in_specs=in_specs,
        out_specs=pl.BlockSpec((tm, d_model), token_tile),
        out_shape=jax.ShapeDtypeStruct((t, d_model), F32),
        scratch_shapes=[pltpu.VMEM((k, d_model), BF16),
                        pltpu.VMEM((2, ATT_HEADS, tm, HEAD_DIM), F32), pltpu.VMEM((2, tm, HEAD_DIM), F32),
                        pltpu.VMEM((tm, k), BF16)],
        compiler_params=_params(("arbitrary", "arbitrary")),
        name="merge_out_proj",
    )(*outs, *lses, h, gain.reshape(1, d_model), w)


def _out_proj_kernel(a_ref, h_ref, g_ref, w_ref, out_ref, wb_ref):
    _cast_weight_once(w_ref, wb_ref, pl.program_id(0) == 0)
    _project_norm_residual(a_ref, wb_ref, h_ref, g_ref, out_ref)


def _out_proj(h, gain, w, layer, mixed):
    t, d_model = h.shape
    k = w.shape[1]
    tm = OUT_TILE
    row = lambda i: (i, 0)
    fixed = lambda i: (0, 0)
    return pl.pallas_call(
        _out_proj_kernel,
        grid=(t // tm,),
        in_specs=[pl.BlockSpec((tm, k), row), pl.BlockSpec((tm, d_model), row),
                  pl.BlockSpec((1, d_model), fixed),
                  pl.BlockSpec((None, k, d_model), lambda i: (layer, 0, 0), pipeline_mode=SINGLE)],
        out_specs=pl.BlockSpec((tm, d_model), row),
        out_shape=jax.ShapeDtypeStruct((t, d_model), F32),
        scratch_shapes=[pltpu.VMEM((k, d_model), BF16)],
        compiler_params=_params(("arbitrary",)),
        name="out_proj",
    )(mixed, h, gain.reshape(1, d_model), w)


def _mlp_kernel(h_ref, g_in_ref, g_out_ref, w1_ref, w2_ref, out_ref, un_ref):
    f = pl.program_id(1)

    @pl.when(f == 0)
    def _():
        un_ref[...] = _rms_scale(h_ref[...], g_in_ref[...]).astype(BF16)
        out_ref[...] = jnp.zeros_like(out_ref)

    a = jnp.dot(un_ref[...], w1_ref[...].astype(BF16), preferred_element_type=F32)
    a = jnp.square(jnp.maximum(a, 0.0))
    out_ref[...] += jnp.dot(a.astype(BF16), w2_ref[...].astype(BF16), preferred_element_type=F32)

    @pl.when(f == pl.num_programs(1) - 1)
    def _():
        out_ref[...] = h_ref[...] + _rms_scale(out_ref[...], g_out_ref[...])


def _mlp(h, g_in, g_out, w1, w2, layer):
    t, d_model = h.shape
    d_ff = w1.shape[2]
    tm, tf = MLP_TILE_M, MLP_TILE_F
    return pl.pallas_call(
        _mlp_kernel,
        grid=(t // tm, d_ff // tf),
        in_specs=[pl.BlockSpec((tm, d_model), lambda m, f: (m, 0), pipeline_mode=SINGLE),
                  pl.BlockSpec((1, d_model), lambda m, f: (0, 0)),
                  pl.BlockSpec((1, d_model), lambda m, f: (0, 0)),
                  pl.BlockSpec((None, d_model, tf), lambda m, f: (layer, 0, f)),
                  pl.BlockSpec((None, tf, d_model), lambda m, f: (layer, f, 0))],
        out_specs=pl.BlockSpec((tm, d_model), lambda m, f: (m, 0)),
        out_shape=jax.ShapeDtypeStruct((t, d_model), F32),
        scratch_shapes=[pltpu.VMEM((tm, d_model), BF16)],
        compiler_params=_params(("parallel", "arbitrary")),
        name="mlp",
    )(h, g_in.reshape(1, d_model), g_out.reshape(1, d_model), w1, w2)


def _hgrn_constants():
    r = np.arange(HGRN_CHUNK)[:, None]
    c = np.arange(HGRN_CHUNK)[None, :]
    tri = (c <= r).astype(np.float32)
    msb = np.floor(np.log2(np.maximum(r ^ c, 1))).astype(np.int32)
    lvl = np.where(c < r, msb, -1).astype(np.int32)
    sgn = np.concatenate([np.where((r >> j) & 1 == 1, LOG2E, -LOG2E) * np.ones_like(c)
                          for j in range(2, HGRN_LEVELS)], axis=0).astype(np.float32)
    return tri, lvl, sgn


def _hgrn_kernel(q_ref, i_ref, gt_ref, logf_ref, kk_ref, gn_ref, tri_ref, lvl_ref, sgn_ref, o_ref, st_ref):
    nh = HGRN_HEADS_PER_STEP
    cs = HGRN_CHUNK
    width = nh * HEAD_DIM

    @pl.when(pl.program_id(2) == 0)
    def _():
        st_ref[...] = jnp.zeros_like(st_ref)

    log_f = logf_ref[...]

    g_hi = log_f.astype(BF16)
    r1 = log_f - g_hi.astype(F32)
    g_mid = r1.astype(BF16)
    g_lo = (r1 - g_mid.astype(F32)).astype(BF16)
    tri = tri_ref[...]
    b = (jnp.dot(tri, g_hi, preferred_element_type=F32)
         + jnp.dot(tri, g_mid, preferred_element_type=F32)
         + jnp.dot(tri, g_lo, preferred_element_type=F32))
    b_last = b[cs - 1:cs, :]

    row = lax.broadcasted_iota(jnp.int32, (cs, width), 0)
    up = pltpu.roll(log_f, cs - 1, 0)
    down = pltpu.roll(log_f, 1, 0)
    r4 = row & 3
    level_e = [
        jnp.exp(jnp.where((row & 1) == 1, log_f, 0.0)),
        jnp.exp(jnp.where(r4 == 0, up, jnp.where(r4 == 1, 0.0, jnp.where(r4 == 2, log_f, log_f + down)))),
    ]
    for j in range(2, HGRN_LEVELS):
        half = 1 << j
        nblk = cs // (2 * half)
        b3 = b.reshape(nblk, 2 * half, width)
        mid = jnp.broadcast_to(b3[:, half - 1:half, :], b3.shape).reshape(cs, width)
        sign_log2e = jnp.tile(sgn_ref[(j - 2) * cs:(j - 1) * cs, :], (1, nh))
        level_e.append(jnp.exp2((b - mid) * sign_log2e))
    e_incl = jnp.exp(b)
    e_suffix = jnp.exp(b_last - b)

    lvl = lvl_ref[...]
    level_mask = [lvl == j for j in range(HGRN_LEVELS)]
    nt = (((1,), (1,)), ((), ()))
    tn = (((0,), (0,)), ((), ()))
    for hh in range(nh):
        sl = slice(hh * HEAD_DIM, (hh + 1) * HEAD_DIM)
        qb = q_ref[:, sl]
        qh = qb.astype(F32)
        kb = kk_ref[:, sl]
        kh = kb.astype(F32)
        vb = i_ref[:, sl]
        vh = vb.astype(F32)
        a_mat = jnp.zeros((cs, cs), F32)
        for j in range(HGRN_LEVELS):
            ej = level_e[j][:, sl].astype(BF16)
            aj = lax.dot_general(qb * ej, kb * ej, nt, preferred_element_type=F32)
            a_mat = jnp.where(level_mask[j], aj, a_mat)
        q_dec = qb * e_incl[:, sl].astype(BF16)
        k_dec = kb * e_suffix[:, sl].astype(BF16)
        st = st_ref[hh]
        inter = lax.dot_general(q_dec, st.astype(BF16), nt, preferred_element_type=F32)
        intra = jnp.dot(a_mat.astype(BF16), vb, preferred_element_type=F32)
        diag = jnp.sum(qh * kh, axis=-1, keepdims=True) * vh
        o = inter + intra + diag
        st_ref[hh] = st * e_incl[cs - 1:cs, sl] + lax.dot_general(vb, k_dec, tn, preferred_element_type=F32)
        gt = gt_ref[:, sl].astype(F32)
        on = _rms_scale(o, gn_ref[...]) * (gt * (1.0 / (1.0 + jnp.exp(-gt))))
        o_ref[:, sl] = on.astype(o_ref.dtype)


def _hgrn_recurrence(qig, log_f, kk, out_norm_gain, batch, seq, d_model):
    wb = HGRN_HEADS_PER_STEP * HEAD_DIM
    hb = d_model // wb
    tri, lvl, sgn = _hgrn_constants()

    def part(k):
        return pl.BlockSpec((None, None, HGRN_CHUNK, wb), lambda b, g, c: (k, b, c, g))

    rows = pl.BlockSpec((None, HGRN_CHUNK, wb), lambda b, g, c: (b, c, g))

    fixed = lambda b, g, c: (0, 0)
    out = pl.pallas_call(
        _hgrn_kernel,
        grid=(batch, hb, seq // HGRN_CHUNK),
        in_specs=[part(0), part(1), part(2), rows, rows,
                  pl.BlockSpec((1, HEAD_DIM), fixed),
                  pl.BlockSpec(tri.shape, fixed),
                  pl.BlockSpec(lvl.shape, fixed),
                  pl.BlockSpec(sgn.shape, fixed)],
        out_specs=rows,
        out_shape=jax.ShapeDtypeStruct((batch, seq, d_model), BF16),
        scratch_shapes=[pltpu.VMEM((HGRN_HEADS_PER_STEP, HEAD_DIM, HEAD_DIM), F32)],
        compiler_params=_params(("parallel", "parallel", "arbitrary")),
        name="hgrn_recurrence",
    )(qig, qig, qig, log_f, kk, out_norm_gain.reshape(1, HEAD_DIM),
      jnp.asarray(tri, BF16), jnp.asarray(lvl), jnp.asarray(sgn))
    return out.reshape(batch * seq, d_model)


@jax.jit
def kernel(x, positions, norm_gains, w_att_in, w_att_out, w_rec_in, rec_lower_bounds, rec_out_norm,
           w_rec_out, w_ff1, w_ff2):
    batch, seq, d_model = x.shape
    depth = norm_gains.shape[0]
    tables = [_rope_tables(positions, dilation) for _, dilation in DILATED_GROUPS]
    h = x.reshape(batch * seq, d_model)
    for layer in range(depth):
        g = norm_gains[layer]
        j = layer // 2
        if layer % 2 == 0:
            outs, lses = [], []
            for gi, (window, dilation) in enumerate(DILATED_GROUPS):
                assert window // dilation == ATT_BLOCK
                qkv = _att_in_proj(h, g[0], w_att_in, j, gi, dilation, batch, seq, tables[gi])
                o, lse = _attention(qkv, dilation, batch, seq)
                outs.append(o)
                lses.append(lse)
            h = _merge_out_proj(h, g[1], w_att_out, j, outs, lses, batch, seq)
        else:
            qig = _part_proj(h, g[0], w_rec_in, j, (0, 2, 3))
            log_f, kk = _forget_proj(h, g[0], w_rec_in, j, 1, rec_lower_bounds)
            mixed = _hgrn_recurrence(qig.reshape(3, batch, seq, d_model), log_f.reshape(batch, seq, d_model),
                                     kk.reshape(batch, seq, d_model), rec_out_norm[j], batch, seq, d_model)
            h = _out_proj(h, g[1], w_rec_out, j, mixed)
        h = _mlp(h, g[2], g[3], w_ff1, w_ff2, layer)
    return h.reshape(batch, seq, d_model)
```

```python
import functools

import numpy as np
import jax
import jax.numpy as jnp
from jax import lax
from jax.experimental import pallas as pl
from jax.experimental.pallas import tpu as pltpu

F32 = jnp.float32
BF16 = jnp.bfloat16

NORM_EPS = 1e-6
MASK_VALUE = -1e30
LB_FLOOR = 1e-30

DILATED_GROUPS = ((128, 1), (512, 4), (2048, 16))
HEAD_DIM = 128
ATT_HEADS = 8
ATT_BLOCK = 128
ATT_GROUP_WIDTH = ATT_HEADS * HEAD_DIM
ROPE_THETA = 500000.0
ROPE_DIM = HEAD_DIM // 4
ROPE_HALF = ROPE_DIM // 2
TOKEN_TILE = 256
PROJ_TILE_M, PROJ_SUB = 512, 256
OUT_TILE, OUT_SUB = 512, 256
MLP_TILE_M, MLP_TILE_F = 1024, 512
HGRN_CHUNK = 128
HGRN_LEVELS = 7
HGRN_HEADS_PER_STEP = 8
HGRN_GROUP = 2
LOG2E = 1.4426950408889634

VMEM_LIMIT_BYTES = 56 * 1024 * 1024
SINGLE = pl.Buffered(1)


def _params(semantics):
    return pltpu.CompilerParams(dimension_semantics=semantics, vmem_limit_bytes=VMEM_LIMIT_BYTES)


def _rms_scale(x, gain):
    ms = jnp.mean(x * x, axis=-1, keepdims=True)
    return x * lax.rsqrt(ms + NORM_EPS) * gain


def _rope_kernel(pos_ref, invf_ref, cos_ref, sa_ref, sb_ref):
    ang = pos_ref[...].astype(F32) * invf_ref[...]
    lane = lax.broadcasted_iota(jnp.int32, ang.shape, 1)
    c = jnp.cos(ang)
    s = jnp.sin(ang)
    cos_ref[...] = jnp.where(lane < ROPE_DIM, c, 1.0)
    sa_ref[...] = jnp.where((lane >= ROPE_HALF) & (lane < ROPE_DIM), s, 0.0)
    sb_ref[...] = jnp.where(lane < ROPE_HALF, -s, 0.0)


def _rope_tables(positions, dilation):
    batch, seq = positions.shape
    t = batch * seq
    tr = 1024
    pos = positions.reshape(batch, seq // dilation, dilation).transpose(0, 2, 1)
    inv_freq = ROPE_THETA ** (-jnp.arange(ROPE_HALF, dtype=F32) / ROPE_HALF)
    invf = jnp.zeros((1, HEAD_DIM), F32).at[0, :ROPE_DIM].set(jnp.tile(inv_freq, 2))
    out = jax.ShapeDtypeStruct((t, HEAD_DIM), F32)
    tabs = pl.pallas_call(
        _rope_kernel,
        grid=(t // tr,),
        in_specs=[pl.BlockSpec((tr, 1), lambda i: (i, 0)),
                  pl.BlockSpec((1, HEAD_DIM), lambda i: (0, 0))],
        out_specs=[pl.BlockSpec((tr, HEAD_DIM), lambda i: (i, 0))] * 3,
        out_shape=[out, out, out],
        compiler_params=_params(("parallel",)),
        name="rope_tables",
    )(pos.reshape(t, 1), invf)
    return [tab.reshape(batch, dilation, seq // dilation, HEAD_DIM) for tab in tabs]


def _att_in_proj_kernel(*refs, dilation):
    refs = list(refs)
    planes_ref = refs.pop() if dilation > 1 else None
    x_ref, g_ref, w_ref, cos_ref, sa_ref, sb_ref, o_ref, wb_ref, xn_ref = refs
    tm, d_model = x_ref.shape
    n = tm // dilation

    _cast_rows_once(w_ref, wb_ref, (pl.program_id(0) == 0) & (pl.program_id(1) == 0))
    xn = _rms_scale(x_ref[...], g_ref[...])
    if dilation == 1:
        xn_ref[...] = xn.astype(BF16)
    else:
        for c in range(d_model // HEAD_DIM):
            planes_ref[c] = xn[:, c * HEAD_DIM:(c + 1) * HEAD_DIM]
        for r in range(dilation):
            for c in range(d_model // HEAD_DIM):
                rows = planes_ref[c, pl.ds(r, n, stride=dilation), :]
                xn_ref[r * n:(r + 1) * n, c * HEAD_DIM:(c + 1) * HEAD_DIM] = rows.astype(BF16)

    acc = jnp.dot(xn_ref[...], wb_ref[...], preferred_element_type=F32)

    def store(lo, val):
        for r in range(dilation):
            o_ref[r, :, lo:lo + val.shape[1]] = val[r * n:(r + 1) * n].astype(o_ref.dtype)

    cos, sa, sb = (ref[...].reshape(tm, HEAD_DIM) for ref in (cos_ref, sa_ref, sb_ref))
    scale = HEAD_DIM ** -0.5
    for part, mult in ((0, scale), (1, 1.0)):
        c, a, b = cos * mult, sa * mult, sb * mult
        for hd in range(ATT_HEADS):
            lo = part * ATT_GROUP_WIDTH + hd * HEAD_DIM
            xh = acc[:, lo:lo + HEAD_DIM]
            rot = (xh * c + pltpu.roll(xh, ROPE_HALF, 1) * a
                   + pltpu.roll(xh, HEAD_DIM - ROPE_HALF, 1) * b)
            store(lo, rot)
    store(2 * ATT_GROUP_WIDTH, acc[:, 2 * ATT_GROUP_WIDTH:])


def _att_in_proj(h, gain, w, layer, group, dilation, batch, seq, tables):
    t, d_model = h.shape
    tm = TOKEN_TILE
    tiles = seq // tm
    n = tm // dilation
    width = 3 * ATT_GROUP_WIDTH
    residue_tile = lambda b, i: (b, 0, i, 0)
    scratch = [pltpu.VMEM((d_model, width), BF16), pltpu.VMEM((tm, d_model), BF16)]
    if dilation > 1:
        scratch.append(pltpu.VMEM((d_model // HEAD_DIM, tm, HEAD_DIM), F32))
    return pl.pallas_call(
        functools.partial(_att_in_proj_kernel, dilation=dilation),
        grid=(batch, tiles),
        in_specs=[pl.BlockSpec((tm, d_model), lambda b, i: (b * tiles + i, 0)),
                  pl.BlockSpec((1, d_model), lambda b, i: (0, 0)),
                  pl.BlockSpec((None, d_model, width), lambda b, i: (layer, 0, group), pipeline_mode=SINGLE)]
                 + [pl.BlockSpec((None, dilation, n, HEAD_DIM), residue_tile)] * 3,
        out_specs=pl.BlockSpec((None, dilation, n, width), residue_tile),
        out_shape=jax.ShapeDtypeStruct((batch, dilation, seq // dilation, width), BF16),
        scratch_shapes=scratch,
        compiler_params=_params(("arbitrary", "arbitrary")),
        name="att_in_proj",
    )(h, gain.reshape(1, d_model), w, *tables)


def _cast_rows_once(w_ref, wb_ref, first):
    @pl.when(first)
    def _():
        def cast_rows(i, carry):
            rows = pl.ds(pl.multiple_of(i * HEAD_DIM, HEAD_DIM), HEAD_DIM)
            wb_ref[rows, :] = w_ref[rows, :].astype(BF16)
            return carry
        lax.fori_loop(0, w_ref.shape[0] // HEAD_DIM, cast_rows, 0)


def _part_proj_kernel(x_ref, g_ref, w_ref, o_ref, wb_ref):
    _cast_rows_once(w_ref, wb_ref, pl.program_id(1) == 0)
    for lo in range(0, x_ref.shape[0], PROJ_SUB):
        rows = slice(lo, lo + PROJ_SUB)
        xn = _rms_scale(x_ref[rows, :], g_ref[...]).astype(BF16)
        o_ref[rows, :] = jnp.dot(xn, wb_ref[...], preferred_element_type=F32).astype(o_ref.dtype)


def _forget_proj_kernel(x_ref, g_ref, w_ref, lbp_ref, logf_ref, kk_ref, wb_ref, *, layer_j):
    _cast_rows_once(w_ref, wb_ref, pl.program_id(0) == 0)
    lbp = lbp_ref[...]
    e = jnp.exp(lbp - jnp.max(lbp, axis=0, keepdims=True))
    p = e / jnp.sum(e, axis=0, keepdims=True)
    csum = p[0:1]
    for t in range(1, layer_j + 1):
        csum = csum + p[t:t + 1]
    lb = csum - p[0:1]
    lb_floor = jnp.maximum(lb, LB_FLOOR)
    one_minus_lb = 1.0 - lb
    for lo in range(0, x_ref.shape[0], PROJ_SUB):
        rows = slice(lo, lo + PROJ_SUB)
        xn = _rms_scale(x_ref[rows, :], g_ref[...]).astype(BF16)
        z = jnp.dot(xn, wb_ref[...], preferred_element_type=F32)
        en = jnp.exp(-jnp.abs(z))
        rcp = 1.0 / (1.0 + en)
        pos = z >= 0.0
        logf_ref[rows, :] = jnp.log(lb_floor + one_minus_lb * (jnp.where(pos, 1.0, en) * rcp))
        kk_ref[rows, :] = (one_minus_lb * (jnp.where(pos, en, 1.0) * rcp) - (lb_floor - lb)).astype(kk_ref.dtype)


def _part_proj(h, gain, w, layer, parts):
    t, d_model = h.shape
    tm = PROJ_TILE_M

    def part_of(p):
        idx = parts[-1]
        for k in range(len(parts) - 2, -1, -1):
            idx = jnp.where(p == k, parts[k], idx)
        return idx

    return pl.pallas_call(
        _part_proj_kernel,
        grid=(len(parts), t // tm),
        in_specs=[pl.BlockSpec((tm, d_model), lambda p, i: (i, 0)),
                  pl.BlockSpec((1, d_model), lambda p, i: (0, 0)),
                  pl.BlockSpec((None, d_model, d_model), lambda p, i: (layer, 0, part_of(p)),
                               pipeline_mode=SINGLE)],
        out_specs=pl.BlockSpec((None, tm, d_model), lambda p, i: (p, i, 0)),
        out_shape=jax.ShapeDtypeStruct((len(parts), t, d_model), BF16),
        scratch_shapes=[pltpu.VMEM((d_model, d_model), BF16)],
        compiler_params=_params(("arbitrary", "arbitrary")),
        name="part_proj",
    )(h, gain.reshape(1, d_model), w)


def _forget_proj(h, gain, w, layer, part, lower_bound_params):
    t, d_model = h.shape
    tm = PROJ_TILE_M
    n_layers = lower_bound_params.shape[0]
    row = lambda i: (i, 0)
    fixed = lambda i: (0, 0)
    return pl.pallas_call(
        functools.partial(_forget_proj_kernel, layer_j=layer),
        grid=(t // tm,),
        in_specs=[pl.BlockSpec((tm, d_model), row),
                  pl.BlockSpec((1, d_model), fixed),
                  pl.BlockSpec((None, d_model, d_model), lambda i: (layer, 0, part), pipeline_mode=SINGLE),
                  pl.BlockSpec((n_layers, d_model), fixed)],
        out_specs=[pl.BlockSpec((tm, d_model), row), pl.BlockSpec((tm, d_model), row)],
        out_shape=[jax.ShapeDtypeStruct((t, d_model), F32), jax.ShapeDtypeStruct((t, d_model), BF16)],
        scratch_shapes=[pltpu.VMEM((d_model, d_model), BF16)],
        compiler_params=_params(("arbitrary",)),
        name="forget_proj",
    )(h, gain.reshape(1, d_model), w, lower_bound_params)


def _attention_kernel(*refs, has_prev):
    if has_prev:
        q_ref, kc_ref, vc_ref, o_ref, lse_ref, s_ref, prev_ref = refs
        kp_ref, vp_ref = prev_ref.at[0], prev_ref.at[1]
        first_block = pl.program_id(2) == 0

        @pl.when(first_block)
        def _():
            prev_ref[...] = jnp.zeros_like(prev_ref)
    else:
        q_ref, kc_ref, vc_ref, o_ref, lse_ref, s_ref = refs
    n_keys = s_ref.shape[2]
    row = lax.broadcasted_iota(jnp.int32, (ATT_BLOCK, n_keys), 0)
    col = lax.broadcasted_iota(jnp.int32, (ATT_BLOCK, n_keys), 1)
    if has_prev:
        mask = (col >= row) & (col <= row + ATT_BLOCK) & ((col >= ATT_BLOCK) | jnp.logical_not(first_block))
    else:
        mask = col <= row
    nt = (((1,), (1,)), ((), ()))
    heads = [slice(hd * HEAD_DIM, (hd + 1) * HEAD_DIM) for hd in range(ATT_HEADS)]

    def both(prev_ref, cur_ref, sl):
        if has_prev:
            return jnp.concatenate([prev_ref[:, sl], cur_ref[:, sl]], axis=0)
        return cur_ref[:, sl]

    maxes = []
    for hd, sl in enumerate(heads):
        s = lax.dot_general(q_ref[:, sl], both(kp_ref if has_prev else None, kc_ref, sl), nt,
                            preferred_element_type=F32)
        s = jnp.where(mask, s, MASK_VALUE)
        s_ref[hd] = s
        maxes.append(jnp.max(s, axis=-1, keepdims=True))
    lane = lax.broadcasted_iota(jnp.int32, (ATT_BLOCK, HEAD_DIM), 1)
    lse_tile = jnp.zeros((ATT_BLOCK, HEAD_DIM), F32)
    for hd, sl in enumerate(heads):
        p = jnp.exp(s_ref[hd] - maxes[hd])
        l = jnp.sum(p, axis=-1, keepdims=True)
        acc = jnp.dot(p.astype(BF16), both(vp_ref if has_prev else None, vc_ref, sl),
                      preferred_element_type=F32)
        o_ref[:, sl] = (acc / l).astype(o_ref.dtype)
        lse_tile = jnp.where(lane == hd, maxes[hd] + jnp.log(l), lse_tile)
    lse_ref[...] = lse_tile
    if has_prev:
        kp_ref[...] = kc_ref[...]
        vp_ref[...] = vc_ref[...]


def _attention(qkv, dilation, batch, seq):
    length = seq // dilation
    nb = length // ATT_BLOCK
    has_prev = nb > 1
    blk_shape = (None, None, ATT_BLOCK, ATT_GROUP_WIDTH)

    def cur(part):
        return pl.BlockSpec(blk_shape, lambda b, r, i: (b, r, i, part))

    in_specs = [cur(0), cur(1), cur(2)]
    n_keys = 2 * ATT_BLOCK if has_prev else ATT_BLOCK
    scratch = [pltpu.VMEM((ATT_HEADS, ATT_BLOCK, n_keys), F32)]
    if has_prev:
        scratch.append(pltpu.VMEM((2, ATT_BLOCK, ATT_GROUP_WIDTH), BF16))
    return pl.pallas_call(
        functools.partial(_attention_kernel, has_prev=has_prev),
        grid=(batch, dilation, nb),
        in_specs=in_specs,
        out_specs=[pl.BlockSpec(blk_shape, lambda b, r, i: (b, r, i, 0)),
                   pl.BlockSpec((None, None, ATT_BLOCK, HEAD_DIM), lambda b, r, i: (b, r, i, 0))],
        out_shape=[jax.ShapeDtypeStruct((batch, dilation, length, ATT_GROUP_WIDTH), BF16),
                   jax.ShapeDtypeStruct((batch, dilation, length, HEAD_DIM), F32)],
        scratch_shapes=scratch,
        compiler_params=_params(("parallel", "parallel", "arbitrary")),
        name="banded_attention",
    )(*([qkv] * len(in_specs)))


def _cast_weight_once(w_ref, wb_ref, first):
    @pl.when(first)
    def _():
        wb_ref[...] = w_ref[...].astype(BF16)


def _project_norm_residual(a_ref, wb_ref, h_ref, g_ref, out_ref):
    for lo in range(0, a_ref.shape[0], OUT_SUB):
        rows = slice(lo, lo + OUT_SUB)
        y = jnp.dot(a_ref[rows, :], wb_ref[...], preferred_element_type=F32)
        out_ref[rows, :] = h_ref[rows, :] + _rms_scale(y, g_ref[...])


def _merge_out_proj_kernel(o1, o2, o3, l1, l2, l3, h_ref, g_ref, w_ref, out_ref, wb_ref, ot_ref, lt_ref, a_ref):
    _cast_weight_once(w_ref, wb_ref, (pl.program_id(0) == 0) & (pl.program_id(1) == 0))
    tm = h_ref.shape[0]
    for gi, (o_g, l_g) in enumerate(((o2, l2), (o3, l3))):
        dilation = o_g.shape[0]
        n = tm // dilation
        for r in range(dilation):
            rows = pl.ds(r, n, stride=dilation)
            lt_ref[gi, rows, :] = l_g[r]
            for hd in range(ATT_HEADS):
                ot_ref[gi, hd, rows, :] = o_g[r, :, hd * HEAD_DIM:(hd + 1) * HEAD_DIM].astype(F32)
    lses = (l1[0], lt_ref[0], lt_ref[1])
    m = jnp.maximum(jnp.maximum(lses[0], lses[1]), lses[2])
    es = [jnp.exp(v - m) for v in lses]
    den = es[0] + es[1] + es[2]
    ws = [e / den for e in es]
    for hd in range(ATT_HEADS):
        sl = slice(hd * HEAD_DIM, (hd + 1) * HEAD_DIM)
        acc = ws[0][:, hd:hd + 1] * o1[0, :, sl].astype(F32)
        acc = acc + ws[1][:, hd:hd + 1] * ot_ref[0, hd]
        acc = acc + ws[2][:, hd:hd + 1] * ot_ref[1, hd]
        a_ref[:, sl] = acc.astype(BF16)
    _project_norm_residual(a_ref, wb_ref, h_ref, g_ref, out_ref)


def _merge_out_proj(h, gain, w, layer, outs, lses, batch, seq):
    t, d_model = h.shape
    k = w.shape[1]
    tm = OUT_TILE
    tiles = seq // tm
    token_tile = lambda b, i: (b * tiles + i, 0)
    fixed = lambda b, i: (0, 0)

    def group_spec(arr):
        dilation, width = arr.shape[1], arr.shape[3]
        return pl.BlockSpec((None, dilation, tm // dilation, width), lambda b, i: (b, 0, i, 0))

    in_specs = [group_spec(a) for a in outs] + [group_spec(a) for a in lses]
    in_specs += [pl.BlockSpec((tm, d_model), token_tile), pl.BlockSpec((1, d_model), fixed),
                 pl.BlockSpec((None, k, d_model), lambda b, i: (layer, 0, 0), pipeline_mode=SINGLE)]
    return pl.pallas_call(
        _merge_out_proj_kernel,
        grid=(batch, tiles),
        in_specs=in_specs,
        out_specs=pl.BlockSpec((tm, d_model), token_tile),
        out_shape=jax.ShapeDtypeStruct((t, d_model), F32),
        scratch_shapes=[pltpu.VMEM((k, d_model), BF16),
                        pltpu.VMEM((2, ATT_HEADS, tm, HEAD_DIM), F32), pltpu.VMEM((2, tm, HEAD_DIM), F32),
                        pltpu.VMEM((tm, k), BF16)],
        compiler_params=_params(("arbitrary", "arbitrary")),
        name="merge_out_proj",
    )(*outs, *lses, h, gain.reshape(1, d_model), w)


def _out_proj_kernel(a_ref, h_ref, g_ref, w_ref, out_ref, wb_ref):
    _cast_weight_once(w_ref, wb_ref, pl.program_id(0) == 0)
    _project_norm_residual(a_ref, wb_ref, h_ref, g_ref, out_ref)


def _out_proj(h, gain, w, layer, mixed):
    t, d_model = h.shape
    k = w.shape[1]
    tm = OUT_TILE
    row = lambda i: (i, 0)
    fixed = lambda i: (0, 0)
    return pl.pallas_call(
        _out_proj_kernel,
        grid=(t // tm,),
        in_specs=[pl.BlockSpec((tm, k), row), pl.BlockSpec((tm, d_model), row),
                  pl.BlockSpec((1, d_model), fixed),
                  pl.BlockSpec((None, k, d_model), lambda i: (layer, 0, 0), pipeline_mode=SINGLE)],
        out_specs=pl.BlockSpec((tm, d_model), row),
        out_shape=jax.ShapeDtypeStruct((t, d_model), F32),
        scratch_shapes=[pltpu.VMEM((k, d_model), BF16)],
        compiler_params=_params(("arbitrary",)),
        name="out_proj",
    )(mixed, h, gain.reshape(1, d_model), w)


def _mlp_kernel(h_ref, g_in_ref, g_out_ref, w1_ref, w2_ref, out_ref, un_ref):
    f = pl.program_id(1)

    @pl.when(f == 0)
    def _():
        un_ref[...] = _rms_scale(h_ref[...], g_in_ref[...]).astype(BF16)
        out_ref[...] = jnp.zeros_like(out_ref)

    tf = w2_ref.shape[0]
    for half in range(2):
        @pl.when(f % 2 == half)
        def _():
            w1 = w1_ref[:, half * tf:(half + 1) * tf].astype(BF16)
            a = jnp.dot(un_ref[...], w1, preferred_element_type=F32)
            a = jnp.square(jnp.maximum(a, 0.0))
            out_ref[...] += jnp.dot(a.astype(BF16), w2_ref[...].astype(BF16), preferred_element_type=F32)

    @pl.when(f == pl.num_programs(1) - 1)
    def _():
        out_ref[...] = h_ref[...] + _rms_scale(out_ref[...], g_out_ref[...])


def _mlp(h, g_in, g_out, w1, w2, layer):
    t, d_model = h.shape
    d_ff = w1.shape[2]
    tm, tf = MLP_TILE_M, MLP_TILE_F
    return pl.pallas_call(
        _mlp_kernel,
        grid=(t // tm, d_ff // tf),
        in_specs=[pl.BlockSpec((tm, d_model), lambda m, f: (m, 0), pipeline_mode=SINGLE),
                  pl.BlockSpec((1, d_model), lambda m, f: (0, 0)),
                  pl.BlockSpec((1, d_model), lambda m, f: (0, 0)),
                  pl.BlockSpec((None, d_model, 2 * tf), lambda m, f: (layer, 0, f // 2)),
                  pl.BlockSpec((None, tf, d_model), lambda m, f: (layer, f, 0))],
        out_specs=pl.BlockSpec((tm, d_model), lambda m, f: (m, 0), pipeline_mode=SINGLE),
        out_shape=jax.ShapeDtypeStruct((t, d_model), F32),
        scratch_shapes=[pltpu.VMEM((tm, d_model), BF16)],
        compiler_params=_params(("parallel", "arbitrary")),
        name="mlp",
    )(h, g_in.reshape(1, d_model), g_out.reshape(1, d_model), w1, w2)


def _hgrn_constants():
    r = np.arange(HGRN_CHUNK)[:, None]
    c = np.arange(HGRN_CHUNK)[None, :]
    tri = (c <= r).astype(np.float32)
    msb = np.floor(np.log2(np.maximum(r ^ c, 1))).astype(np.int32)
    lvl = np.where(c < r, msb, -1).astype(np.int32)
    sgn = np.concatenate([np.where((r >> j) & 1 == 1, LOG2E, -LOG2E) * np.ones_like(c)
                          for j in range(2, HGRN_LEVELS)], axis=0).astype(np.float32)
    return tri, lvl, sgn


def _hgrn_kernel(q_ref, i_ref, gt_ref, logf_ref, kk_ref, gn_ref, tri_ref, lvl_ref, sgn_ref, o_ref, st_ref):
    nh = HGRN_HEADS_PER_STEP
    cs = HGRN_CHUNK

    @pl.when(pl.program_id(2) == 0)
    def _():
        st_ref[...] = jnp.zeros_like(st_ref)

    tri = tri_ref[...]
    lvl = lvl_ref[...]
    level_mask = [lvl == j for j in range(HGRN_LEVELS)]
    gw = HGRN_GROUP * HEAD_DIM
    row = lax.broadcasted_iota(jnp.int32, (cs, gw), 0)
    r4 = row & 3
    nt = (((1,), (1,)), ((), ()))
    tn = (((0,), (0,)), ((), ()))
    for grp in range(nh // HGRN_GROUP):
        gsl = slice(grp * gw, (grp + 1) * gw)
        log_f = logf_ref[:, gsl]

        g_hi = log_f.astype(BF16)
        r1 = log_f - g_hi.astype(F32)
        g_mid = r1.astype(BF16)
        g_lo = (r1 - g_mid.astype(F32)).astype(BF16)
        b = (jnp.dot(tri, g_hi, preferred_element_type=F32)
             + jnp.dot(tri, g_mid, preferred_element_type=F32)
             + jnp.dot(tri, g_lo, preferred_element_type=F32))
        b_last = b[cs - 1:cs, :]

        up = pltpu.roll(log_f, cs - 1, 0)
        down = pltpu.roll(log_f, 1, 0)
        level_e = [
            jnp.exp(jnp.where((row & 1) == 1, log_f, 0.0)),
            jnp.exp(jnp.where(r4 == 0, up, jnp.where(r4 == 1, 0.0, jnp.where(r4 == 2, log_f, log_f + down)))),
        ]
        for j in range(2, HGRN_LEVELS):
            half = 1 << j
            nblk = cs // (2 * half)
            b3 = b.reshape(nblk, 2 * half, gw)
            mid = jnp.broadcast_to(b3[:, half - 1:half, :], b3.shape).reshape(cs, gw)
            sign_log2e = jnp.tile(sgn_ref[(j - 2) * cs:(j - 1) * cs, :], (1, HGRN_GROUP))
            level_e.append(jnp.exp2((b - mid) * sign_log2e))
        e_incl = jnp.exp(b)
        e_suffix = jnp.exp(b_last - b)

        for hh in range(HGRN_GROUP):
            head = grp * HGRN_GROUP + hh
            sl = slice(head * HEAD_DIM, (head + 1) * HEAD_DIM)
            loc = slice(hh * HEAD_DIM, (hh + 1) * HEAD_DIM)
            qb = q_ref[:, sl]
            qh = qb.astype(F32)
            kb = kk_ref[:, sl]
            kh = kb.astype(F32)
            vb = i_ref[:, sl]
            vh = vb.astype(F32)
            a_mat = jnp.zeros((cs, cs), F32)
            for j in range(HGRN_LEVELS):
                ej = level_e[j][:, loc].astype(BF16)
                aj = lax.dot_general(qb * ej, kb * ej, nt, preferred_element_type=F32)
                a_mat = jnp.where(level_mask[j], aj, a_mat)
            q_dec = qb * e_incl[:, loc].astype(BF16)
            k_dec = kb * e_suffix[:, loc].astype(BF16)
            st = st_ref[head]
            inter = lax.dot_general(q_dec, st.astype(BF16), nt, preferred_element_type=F32)
            intra = jnp.dot(a_mat.astype(BF16), vb, preferred_element_type=F32)
            diag = jnp.sum(qh * kh, axis=-1, keepdims=True) * vh
            o = inter + intra + diag
            st_ref[head] = (st * e_incl[cs - 1:cs, loc]
                            + lax.dot_general(vb, k_dec, tn, preferred_element_type=F32))
            gt = gt_ref[:, sl].astype(F32)
            on = _rms_scale(o, gn_ref[...]) * (gt * (1.0 / (1.0 + jnp.exp(-gt))))
            o_ref[:, sl] = on.astype(o_ref.dtype)


def _hgrn_recurrence(qig, log_f, kk, out_norm_gain, batch, seq, d_model):
    wb = HGRN_HEADS_PER_STEP * HEAD_DIM
    hb = d_model // wb
    tri, lvl, sgn = _hgrn_constants()

    def part(k):
        return pl.BlockSpec((None, None, HGRN_CHUNK, wb), lambda b, g, c: (k, b, c, g))

    rows = pl.BlockSpec((None, HGRN_CHUNK, wb), lambda b, g, c: (b, c, g))

    fixed = lambda b, g, c: (0, 0)
    out = pl.pallas_call(
        _hgrn_kernel,
        grid=(batch, hb, seq // HGRN_CHUNK),
        in_specs=[part(0), part(1), part(2), rows, rows,
                  pl.BlockSpec((1, HEAD_DIM), fixed),
                  pl.BlockSpec(tri.shape, fixed),
                  pl.BlockSpec(lvl.shape, fixed),
                  pl.BlockSpec(sgn.shape, fixed)],
        out_specs=rows,
        out_shape=jax.ShapeDtypeStruct((batch, seq, d_model), BF16),
        scratch_shapes=[pltpu.VMEM((HGRN_HEADS_PER_STEP, HEAD_DIM, HEAD_DIM), F32)],
        compiler_params=_params(("parallel", "parallel", "arbitrary")),
        name="hgrn_recurrence",
    )(qig, qig, qig, log_f, kk, out_norm_gain.reshape(1, HEAD_DIM),
      jnp.asarray(tri, BF16), jnp.asarray(lvl), jnp.asarray(sgn))
    return out.reshape(batch * seq, d_model)


@jax.jit
def kernel(x, positions, norm_gains, w_att_in, w_att_out, w_rec_in, rec_lower_bounds, rec_out_norm,
           w_rec_out, w_ff1, w_ff2):
    batch, seq, d_model = x.shape
    depth = norm_gains.shape[0]
    tables = [_rope_tables(positions, dilation) for _, dilation in DILATED_GROUPS]
    h = x.reshape(batch * seq, d_model)
    for layer in range(depth):
        g = norm_gains[layer]
        j = layer // 2
        if layer % 2 == 0:
            outs, lses = [], []
            for gi, (window, dilation) in enumerate(DILATED_GROUPS):
                assert window // dilation == ATT_BLOCK
                qkv = _att_in_proj(h, g[0], w_att_in, j, gi, dilation, batch, seq, tables[gi])
                o, lse = _attention(qkv, dilation, batch, seq)
                outs.append(o)
                lses.append(lse)
            h = _merge_out_proj(h, g[1], w_att_out, j, outs, lses, batch, seq)
        else:
            qig = _part_proj(h, g[0], w_rec_in, j, (0, 2, 3))
            log_f, kk = _forget_proj(h, g[0], w_rec_in, j, 1, rec_lower_bounds)
            mixed = _hgrn_recurrence(qig.reshape(3, batch, seq, d_model), log_f.reshape(batch, seq, d_model),
                                     kk.reshape(batch, seq, d_model), rec_out_norm[j], batch, seq, d_model)
            h = _out_proj(h, g[1], w_rec_out, j, mixed)
        h = _mlp(h, g[2], g[3], w_ff1, w_ff2, layer)
    return h.reshape(batch, seq, d_model)
```

```python
import functools

import numpy as np
import jax
import jax.numpy as jnp
from jax import lax
from jax.experimental import pallas as pl
from jax.experimental.pallas import tpu as pltpu

F32 = jnp.float32
BF16 = jnp.bfloat16

NORM_EPS = 1e-6
MASK_VALUE = -1e30
LB_FLOOR = 1e-30

DILATED_GROUPS = ((128, 1), (512, 4), (2048, 16))
HEAD_DIM = 128
ATT_HEADS = 8
ATT_BLOCK = 128
ATT_SUBS = 2
ATT_GROUP_WIDTH = ATT_HEADS * HEAD_DIM
ROPE_THETA = 500000.0
ROPE_DIM = HEAD_DIM // 4
ROPE_HALF = ROPE_DIM // 2
TOKEN_TILE = 256
PROJ_TILE_M, PROJ_SUB = 512, 256
OUT_TILE, OUT_SUB = 512, 256
MLP_TILE_M, MLP_TILE_F = 1024, 512
HGRN_CHUNK = 128
HGRN_LEVELS = 7
HGRN_HEADS_PER_STEP = 16
HGRN_GROUP = 16
LOG2E = 1.4426950408889634

VMEM_LIMIT_BYTES = 56 * 1024 * 1024
SINGLE = pl.Buffered(1)


def _params(semantics):
    return pltpu.CompilerParams(dimension_semantics=semantics, vmem_limit_bytes=VMEM_LIMIT_BYTES)


def _rms_scale(x, gain):
    ms = jnp.mean(x * x, axis=-1, keepdims=True)
    return x * lax.rsqrt(ms + NORM_EPS) * gain


def _rope_kernel(pos_ref, invf_ref, cos_ref, sa_ref, sb_ref):
    ang = pos_ref[...].astype(F32) * invf_ref[...]
    lane = lax.broadcasted_iota(jnp.int32, ang.shape, 1)
    c = jnp.cos(ang)
    s = jnp.sin(ang)
    cos_ref[...] = jnp.where(lane < ROPE_DIM, c, 1.0)
    sa_ref[...] = jnp.where((lane >= ROPE_HALF) & (lane < ROPE_DIM), s, 0.0)
    sb_ref[...] = jnp.where(lane < ROPE_HALF, -s, 0.0)


def _rope_tables(positions, dilation):
    batch, seq = positions.shape
    t = batch * seq
    tr = 1024
    pos = positions.reshape(batch, seq // dilation, dilation).transpose(0, 2, 1)
    inv_freq = ROPE_THETA ** (-jnp.arange(ROPE_HALF, dtype=F32) / ROPE_HALF)
    invf = jnp.zeros((1, HEAD_DIM), F32).at[0, :ROPE_DIM].set(jnp.tile(inv_freq, 2))
    out = jax.ShapeDtypeStruct((t, HEAD_DIM), F32)
    tabs = pl.pallas_call(
        _rope_kernel,
        grid=(t // tr,),
        in_specs=[pl.BlockSpec((tr, 1), lambda i: (i, 0)),
                  pl.BlockSpec((1, HEAD_DIM), lambda i: (0, 0))],
        out_specs=[pl.BlockSpec((tr, HEAD_DIM), lambda i: (i, 0))] * 3,
        out_shape=[out, out, out],
        compiler_params=_params(("parallel",)),
        name="rope_tables",
    )(pos.reshape(t, 1), invf)
    return [tab.reshape(batch, dilation, seq // dilation, HEAD_DIM) for tab in tabs]


def _att_in_proj_kernel(*refs, dilation):
    refs = list(refs)
    planes_ref = refs.pop() if dilation > 1 else None
    x_ref, g_ref, w_ref, cos_ref, sa_ref, sb_ref, o_ref, wb_ref, xn_ref = refs
    tm, d_model = x_ref.shape
    n = tm // dilation

    _cast_rows_once(w_ref, wb_ref, (pl.program_id(0) == 0) & (pl.program_id(1) == 0))
    xn = _rms_scale(x_ref[...], g_ref[...])
    if dilation == 1:
        xn_ref[...] = xn.astype(BF16)
    else:
        for c in range(d_model // HEAD_DIM):
            planes_ref[c] = xn[:, c * HEAD_DIM:(c + 1) * HEAD_DIM]
        for r in range(dilation):
            for c in range(d_model // HEAD_DIM):
                rows = planes_ref[c, pl.ds(r, n, stride=dilation), :]
                xn_ref[r * n:(r + 1) * n, c * HEAD_DIM:(c + 1) * HEAD_DIM] = rows.astype(BF16)

    acc = jnp.dot(xn_ref[...], wb_ref[...], preferred_element_type=F32)

    def store(lo, val):
        for r in range(dilation):
            o_ref[r, :, lo:lo + val.shape[1]] = val[r * n:(r + 1) * n].astype(o_ref.dtype)

    cos, sa, sb = (ref[...].reshape(tm, HEAD_DIM) for ref in (cos_ref, sa_ref, sb_ref))
    scale = HEAD_DIM ** -0.5
    for part, mult in ((0, scale), (1, 1.0)):
        c, a, b = cos * mult, sa * mult, sb * mult
        for hd in range(ATT_HEADS):
            lo = part * ATT_GROUP_WIDTH + hd * HEAD_DIM
            xh = acc[:, lo:lo + HEAD_DIM]
            rot = (xh * c + pltpu.roll(xh, ROPE_HALF, 1) * a
                   + pltpu.roll(xh, HEAD_DIM - ROPE_HALF, 1) * b)
            store(lo, rot)
    store(2 * ATT_GROUP_WIDTH, acc[:, 2 * ATT_GROUP_WIDTH:])


def _att_in_proj(h, gain, w, layer, group, dilation, batch, seq, tables):
    t, d_model = h.shape
    tm = TOKEN_TILE
    tiles = seq // tm
    n = tm // dilation
    width = 3 * ATT_GROUP_WIDTH
    residue_tile = lambda b, i: (b, 0, i, 0)
    scratch = [pltpu.VMEM((d_model, width), BF16), pltpu.VMEM((tm, d_model), BF16)]
    if dilation > 1:
        scratch.append(pltpu.VMEM((d_model // HEAD_DIM, tm, HEAD_DIM), F32))
    return pl.pallas_call(
        functools.partial(_att_in_proj_kernel, dilation=dilation),
        grid=(batch, tiles),
        in_specs=[pl.BlockSpec((tm, d_model), lambda b, i: (b * tiles + i, 0)),
                  pl.BlockSpec((1, d_model), lambda b, i: (0, 0)),
                  pl.BlockSpec((None, d_model, width), lambda b, i: (layer, 0, group), pipeline_mode=SINGLE)]
                 + [pl.BlockSpec((None, dilation, n, HEAD_DIM), residue_tile)] * 3,
        out_specs=pl.BlockSpec((None, dilation, n, width), residue_tile),
        out_shape=jax.ShapeDtypeStruct((batch, dilation, seq // dilation, width), BF16),
        scratch_shapes=scratch,
        compiler_params=_params(("arbitrary", "arbitrary")),
        name="att_in_proj",
    )(h, gain.reshape(1, d_model), w, *tables)


def _cast_rows_once(w_ref, wb_ref, first):
    @pl.when(first)
    def _():
        def cast_rows(i, carry):
            rows = pl.ds(pl.multiple_of(i * HEAD_DIM, HEAD_DIM), HEAD_DIM)
            wb_ref[rows, :] = w_ref[rows, :].astype(BF16)
            return carry
        lax.fori_loop(0, w_ref.shape[0] // HEAD_DIM, cast_rows, 0)


def _part_proj_kernel(x_ref, g_ref, w_ref, o_ref, wb_ref):
    _cast_rows_once(w_ref, wb_ref, pl.program_id(1) == 0)
    for lo in range(0, x_ref.shape[0], PROJ_SUB):
        rows = slice(lo, lo + PROJ_SUB)
        xn = _rms_scale(x_ref[rows, :], g_ref[...]).astype(BF16)
        o_ref[rows, :] = jnp.dot(xn, wb_ref[...], preferred_element_type=F32).astype(o_ref.dtype)


def _forget_proj_kernel(x_ref, g_ref, w_ref, lbp_ref, logf_ref, kk_ref, wb_ref, *, layer_j):
    _cast_rows_once(w_ref, wb_ref, pl.program_id(0) == 0)
    lbp = lbp_ref[...]
    e = jnp.exp(lbp - jnp.max(lbp, axis=0, keepdims=True))
    p = e / jnp.sum(e, axis=0, keepdims=True)
    csum = p[0:1]
    for t in range(1, layer_j + 1):
        csum = csum + p[t:t + 1]
    lb = csum - p[0:1]
    lb_floor = jnp.maximum(lb, LB_FLOOR)
    one_minus_lb = 1.0 - lb
    for lo in range(0, x_ref.shape[0], PROJ_SUB):
        rows = slice(lo, lo + PROJ_SUB)
        xn = _rms_scale(x_ref[rows, :], g_ref[...]).astype(BF16)
        z = jnp.dot(xn, wb_ref[...], preferred_element_type=F32)
        en = jnp.exp(-jnp.abs(z))
        rcp = 1.0 / (1.0 + en)
        pos = z >= 0.0
        logf_ref[rows, :] = jnp.log(lb_floor + one_minus_lb * (jnp.where(pos, 1.0, en) * rcp))
        kk_ref[rows, :] = (one_minus_lb * (jnp.where(pos, en, 1.0) * rcp) - (lb_floor - lb)).astype(kk_ref.dtype)


def _part_proj(h, gain, w, layer, parts):
    t, d_model = h.shape
    tm = PROJ_TILE_M

    def part_of(p):
        idx = parts[-1]
        for k in range(len(parts) - 2, -1, -1):
            idx = jnp.where(p == k, parts[k], idx)
        return idx

    return pl.pallas_call(
        _part_proj_kernel,
        grid=(len(parts), t // tm),
        in_specs=[pl.BlockSpec((tm, d_model), lambda p, i: (i, 0)),
                  pl.BlockSpec((1, d_model), lambda p, i: (0, 0)),
                  pl.BlockSpec((None, d_model, d_model), lambda p, i: (layer, 0, part_of(p)),
                               pipeline_mode=SINGLE)],
        out_specs=pl.BlockSpec((None, tm, d_model), lambda p, i: (p, i, 0)),
        out_shape=jax.ShapeDtypeStruct((len(parts), t, d_model), BF16),
        scratch_shapes=[pltpu.VMEM((d_model, d_model), BF16)],
        compiler_params=_params(("arbitrary", "arbitrary")),
        name="part_proj",
    )(h, gain.reshape(1, d_model), w)


def _forget_proj(h, gain, w, layer, part, lower_bound_params):
    t, d_model = h.shape
    tm = PROJ_TILE_M
    n_layers = lower_bound_params.shape[0]
    row = lambda i: (i, 0)
    fixed = lambda i: (0, 0)
    return pl.pallas_call(
        functools.partial(_forget_proj_kernel, layer_j=layer),
        grid=(t // tm,),
        in_specs=[pl.BlockSpec((tm, d_model), row),
                  pl.BlockSpec((1, d_model), fixed),
                  pl.BlockSpec((None, d_model, d_model), lambda i: (layer, 0, part), pipeline_mode=SINGLE),
                  pl.BlockSpec((n_layers, d_model), fixed)],
        out_specs=[pl.BlockSpec((tm, d_model), row), pl.BlockSpec((tm, d_model), row)],
        out_shape=[jax.ShapeDtypeStruct((t, d_model), F32), jax.ShapeDtypeStruct((t, d_model), BF16)],
        scratch_shapes=[pltpu.VMEM((d_model, d_model), BF16)],
        compiler_params=_params(("arbitrary",)),
        name="forget_proj",
    )(h, gain.reshape(1, d_model), w, lower_bound_params)


def _attention_kernel(*refs, has_prev):
    if has_prev:
        q_ref, k_ref, v_ref, o_ref, lse_ref, s_ref, prev_ref = refs
        first_block = pl.program_id(2) == 0

        @pl.when(first_block)
        def _():
            prev_ref[...] = jnp.zeros_like(prev_ref)
    else:
        q_ref, k_ref, v_ref, o_ref, lse_ref, s_ref = refs
    n_keys = s_ref.shape[2]
    row = lax.broadcasted_iota(jnp.int32, (ATT_BLOCK, n_keys), 0)
    col = lax.broadcasted_iota(jnp.int32, (ATT_BLOCK, n_keys), 1)
    if has_prev:
        band = (col >= row) & (col <= row + ATT_BLOCK)
    else:
        band = col <= row
    nt = (((1,), (1,)), ((), ()))
    heads = [slice(hd * HEAD_DIM, (hd + 1) * HEAD_DIM) for hd in range(ATT_HEADS)]
    lane = lax.broadcasted_iota(jnp.int32, (ATT_BLOCK, HEAD_DIM), 1)

    for sub in range(ATT_SUBS):
        rows = slice(sub * ATT_BLOCK, (sub + 1) * ATT_BLOCK)
        if has_prev:
            before = slice((sub - 1) * ATT_BLOCK, sub * ATT_BLOCK)
            mask = band & ((col >= ATT_BLOCK) | jnp.logical_not(first_block)) if sub == 0 else band

            def keys(sl):
                prev = prev_ref[0, :, sl] if sub == 0 else k_ref[before, sl]
                return jnp.concatenate([prev, k_ref[rows, sl]], axis=0)

            def values(sl):
                prev = prev_ref[1, :, sl] if sub == 0 else v_ref[before, sl]
                return jnp.concatenate([prev, v_ref[rows, sl]], axis=0)

            queries = lambda sl: q_ref[rows, sl]
            o_blk, lse_blk = o_ref.at[rows], lse_ref.at[rows]
        else:
            mask = band
            keys = lambda sl: k_ref[sub, :, sl]
            values = lambda sl: v_ref[sub, :, sl]
            queries = lambda sl: q_ref[sub, :, sl]
            o_blk, lse_blk = o_ref.at[sub], lse_ref.at[sub]

        maxes = []
        for hd, sl in enumerate(heads):
            s = lax.dot_general(queries(sl), keys(sl), nt, preferred_element_type=F32)
            s = jnp.where(mask, s, MASK_VALUE)
            s_ref[sub * ATT_HEADS + hd] = s
            maxes.append(jnp.max(s, axis=-1, keepdims=True))
        lse_tile = jnp.zeros((ATT_BLOCK, HEAD_DIM), F32)
        for hd, sl in enumerate(heads):
            p = jnp.exp(s_ref[sub * ATT_HEADS + hd] - maxes[hd])
            l = jnp.sum(p, axis=-1, keepdims=True)
            acc = jnp.dot(p.astype(BF16), values(sl), preferred_element_type=F32)
            o_blk[:, sl] = (acc / l).astype(o_ref.dtype)
            lse_tile = jnp.where(lane == hd, maxes[hd] + jnp.log(l), lse_tile)
        lse_blk[...] = lse_tile
    if has_prev:
        last = slice((ATT_SUBS - 1) * ATT_BLOCK, ATT_SUBS * ATT_BLOCK)
        prev_ref[0] = k_ref[last, :]
        prev_ref[1] = v_ref[last, :]


def _attention(qkv, dilation, batch, seq):
    length = seq // dilation
    nb = length // ATT_BLOCK
    has_prev = nb > 1
    if has_prev:
        grid = (batch, dilation, nb // ATT_SUBS)
        shape = lambda w: (None, None, ATT_SUBS * ATT_BLOCK, w)
    else:
        grid = (batch, dilation // ATT_SUBS, nb)
        shape = lambda w: (None, ATT_SUBS, ATT_BLOCK, w)

    def spec(w, part):
        return pl.BlockSpec(shape(w), lambda b, r, i: (b, r, i, part))

    n_keys = 2 * ATT_BLOCK if has_prev else ATT_BLOCK
    scratch = [pltpu.VMEM((ATT_SUBS * ATT_HEADS, ATT_BLOCK, n_keys), F32)]
    if has_prev:
        scratch.append(pltpu.VMEM((2, ATT_BLOCK, ATT_GROUP_WIDTH), BF16))
    return pl.pallas_call(
        functools.partial(_attention_kernel, has_prev=has_prev),
        grid=grid,
        in_specs=[spec(ATT_GROUP_WIDTH, 0), spec(ATT_GROUP_WIDTH, 1), spec(ATT_GROUP_WIDTH, 2)],
        out_specs=[spec(ATT_GROUP_WIDTH, 0), spec(HEAD_DIM, 0)],
        out_shape=[jax.ShapeDtypeStruct((batch, dilation, length, ATT_GROUP_WIDTH), BF16),
                   jax.ShapeDtypeStruct((batch, dilation, length, HEAD_DIM), F32)],
        scratch_shapes=scratch,
        compiler_params=_params(("parallel", "parallel", "arbitrary")),
        name="banded_attention",
    )(qkv, qkv, qkv)


def _cast_weight_once(w_ref, wb_ref, first):
    @pl.when(first)
    def _():
        wb_ref[...] = w_ref[...].astype(BF16)


def _project_norm_residual(a_ref, wb_ref, h_ref, g_ref, out_ref):
    for lo in range(0, a_ref.shape[0], OUT_SUB):
        rows = slice(lo, lo + OUT_SUB)
        y = jnp.dot(a_ref[rows, :], wb_ref[...], preferred_element_type=F32)
        out_ref[rows, :] = h_ref[rows, :] + _rms_scale(y, g_ref[...])


def _merge_out_proj_kernel(o1, o2, o3, l1, l2, l3, h_ref, g_ref, w_ref, out_ref, wb_ref, ot_ref, lt_ref, a_ref):
    _cast_weight_once(w_ref, wb_ref, (pl.program_id(0) == 0) & (pl.program_id(1) == 0))
    tm = h_ref.shape[0]
    for gi, (o_g, l_g) in enumerate(((o2, l2), (o3, l3))):
        dilation = o_g.shape[0]
        n = tm // dilation
        for r in range(dilation):
            rows = pl.ds(r, n, stride=dilation)
            lt_ref[gi, rows, :] = l_g[r]
            for hd in range(ATT_HEADS):
                ot_ref[gi, hd, rows, :] = o_g[r, :, hd * HEAD_DIM:(hd + 1) * HEAD_DIM].astype(F32)
    lses = (l1[0], lt_ref[0], lt_ref[1])
    m = jnp.maximum(jnp.maximum(lses[0], lses[1]), lses[2])
    es = [jnp.exp(v - m) for v in lses]
    den = es[0] + es[1] + es[2]
    ws = [e / den for e in es]
    for hd in range(ATT_HEADS):
        sl = slice(hd * HEAD_DIM, (hd + 1) * HEAD_DIM)
        acc = ws[0][:, hd:hd + 1] * o1[0, :, sl].astype(F32)
        acc = acc + ws[1][:, hd:hd + 1] * ot_ref[0, hd]
        acc = acc + ws[2][:, hd:hd + 1] * ot_ref[1, hd]
        a_ref[:, sl] = acc.astype(BF16)
    _project_norm_residual(a_ref, wb_ref, h_ref, g_ref, out_ref)


def _merge_out_proj(h, gain, w, layer, outs, lses, batch, seq):
    t, d_model = h.shape
    k = w.shape[1]
    tm = OUT_TILE
    tiles = seq // tm
    token_tile = lambda b, i: (b * tiles + i, 0)
    fixed = lambda b, i: (0, 0)

    def group_spec(arr):
        dilation, width = arr.shape[1], arr.shape[3]
        return pl.BlockSpec((None, dilation, tm // dilation, width), lambda b, i: (b, 0, i, 0))

    in_specs = [group_spec(a) for a in outs] + [group_spec(a) for a in lses]
    in_specs += [pl.BlockSpec((tm, d_model), token_tile), pl.BlockSpec((1, d_model), fixed),
                 pl.BlockSpec((None, k, d_model), lambda b, i: (layer, 0, 0), pipeline_mode=SINGLE)]
    return pl.pallas_call(
        _merge_out_proj_kernel,
        grid=(batch, tiles),
        in_specs=in_specs,
        out_specs=pl.BlockSpec((tm, d_model), token_tile),
        out_shape=jax.ShapeDtypeStruct((t, d_model), F32),
        scratch_shapes=[pltpu.VMEM((k, d_model), BF16),
                        pltpu.VMEM((2, ATT_HEADS, tm, HEAD_DIM), F32), pltpu.VMEM((2, tm, HEAD_DIM), F32),
                        pltpu.VMEM((tm, k), BF16)],
        compiler_params=_params(("arbitrary", "arbitrary")),
        name="merge_out_proj",
    )(*outs, *lses, h, gain.reshape(1, d_model), w)


def _out_proj_kernel(a_ref, h_ref, g_ref, w_ref, out_ref, wb_ref):
    _cast_weight_once(w_ref, wb_ref, pl.program_id(0) == 0)
    _project_norm_residual(a_ref, wb_ref, h_ref, g_ref, out_ref)


def _out_proj(h, gain, w, layer, mixed):
    t, d_model = h.shape
    k = w.shape[1]
    tm = OUT_TILE
    row = lambda i: (i, 0)
    fixed = lambda i: (0, 0)
    return pl.pallas_call(
        _out_proj_kernel,
        grid=(t // tm,),
        in_specs=[pl.BlockSpec((tm, k), row), pl.BlockSpec((tm, d_model), row),
                  pl.BlockSpec((1, d_model), fixed),
                  pl.BlockSpec((None, k, d_model), lambda i: (layer, 0, 0), pipeline_mode=SINGLE)],
        out_specs=pl.BlockSpec((tm, d_model), row),
        out_shape=jax.ShapeDtypeStruct((t, d_model), F32),
        scratch_shapes=[pltpu.VMEM((k, d_model), BF16)],
        compiler_params=_params(("arbitrary",)),
        name="out_proj",
    )(mixed, h, gain.reshape(1, d_model), w)


def _mlp_kernel(h_ref, g_in_ref, g_out_ref, w1_ref, w2_ref, out_ref, un_ref):
    f = pl.program_id(1)

    @pl.when(f == 0)
    def _():
        un_ref[...] = _rms_scale(h_ref[...], g_in_ref[...]).astype(BF16)
        out_ref[...] = jnp.zeros_like(out_ref)

    a = jnp.dot(un_ref[...], w1_ref[...].astype(BF16), preferred_element_type=F32)
    a = jnp.square(jnp.maximum(a, 0.0))
    out_ref[...] += jnp.dot(a.astype(BF16), w2_ref[...].astype(BF16), preferred_element_type=F32)

    @pl.when(f == pl.num_programs(1) - 1)
    def _():
        out_ref[...] = h_ref[...] + _rms_scale(out_ref[...], g_out_ref[...])


def _mlp(h, g_in, g_out, w1, w2, layer):
    t, d_model = h.shape
    d_ff = w1.shape[2]
    tm, tf = MLP_TILE_M, MLP_TILE_F
    return pl.pallas_call(
        _mlp_kernel,
        grid=(t // tm, d_ff // tf),
        in_specs=[pl.BlockSpec((tm, d_model), lambda m, f: (m, 0), pipeline_mode=SINGLE),
                  pl.BlockSpec((1, d_model), lambda m, f: (0, 0)),
                  pl.BlockSpec((1, d_model), lambda m, f: (0, 0)),
                  pl.BlockSpec((None, d_model, tf), lambda m, f: (layer, 0, f)),
                  pl.BlockSpec((None, tf, d_model), lambda m, f: (layer, f, 0))],
        out_specs=pl.BlockSpec((tm, d_model), lambda m, f: (m, 0)),
        out_shape=jax.ShapeDtypeStruct((t, d_model), F32),
        scratch_shapes=[pltpu.VMEM((tm, d_model), BF16)],
        compiler_params=_params(("parallel", "arbitrary")),
        name="mlp",
    )(h, g_in.reshape(1, d_model), g_out.reshape(1, d_model), w1, w2)


def _hgrn_constants():
    r = np.arange(HGRN_CHUNK)[:, None]
    c = np.arange(HGRN_CHUNK)[None, :]
    tri = (c <= r).astype(np.float32)
    msb = np.floor(np.log2(np.maximum(r ^ c, 1))).astype(np.int32)
    lvl = np.where(c < r, msb, -1).astype(np.int32)
    sgn = np.concatenate([np.where((r >> j) & 1 == 1, LOG2E, -LOG2E) * np.ones_like(c)
                          for j in range(2, HGRN_LEVELS)], axis=0).astype(np.float32)
    return tri, lvl, sgn


def _hgrn_kernel(q_ref, i_ref, gt_ref, logf_ref, kk_ref, gn_ref, tri_ref, lvl_ref, sgn_ref, o_ref, st_ref):
    nh = HGRN_HEADS_PER_STEP
    cs = HGRN_CHUNK

    @pl.when(pl.program_id(2) == 0)
    def _():
        st_ref[...] = jnp.zeros_like(st_ref)

    tri = tri_ref[...]
    lvl = lvl_ref[...]
    level_mask = [lvl == j for j in range(HGRN_LEVELS)]
    gw = HGRN_GROUP * HEAD_DIM
    row = lax.broadcasted_iota(jnp.int32, (cs, gw), 0)
    r4 = row & 3
    nt = (((1,), (1,)), ((), ()))
    tn = (((0,), (0,)), ((), ()))
    for grp in range(nh // HGRN_GROUP):
        gsl = slice(grp * gw, (grp + 1) * gw)
        log_f = logf_ref[:, gsl]

        g_hi = log_f.astype(BF16)
        r1 = log_f - g_hi.astype(F32)
        g_mid = r1.astype(BF16)
        g_lo = (r1 - g_mid.astype(F32)).astype(BF16)
        b = (jnp.dot(tri, g_hi, preferred_element_type=F32)
             + jnp.dot(tri, g_mid, preferred_element_type=F32)
             + jnp.dot(tri, g_lo, preferred_element_type=F32))
        b_last = b[cs - 1:cs, :]

        up = pltpu.roll(log_f, cs - 1, 0)
        down = pltpu.roll(log_f, 1, 0)
        level_e = [
            jnp.exp(jnp.where((row & 1) == 1, log_f, 0.0)),
            jnp.exp(jnp.where(r4 == 0, up, jnp.where(r4 == 1, 0.0, jnp.where(r4 == 2, log_f, log_f + down)))),
        ]
        for j in range(2, HGRN_LEVELS):
            half = 1 << j
            nblk = cs // (2 * half)
            b3 = b.reshape(nblk, 2 * half, gw)
            mid = jnp.broadcast_to(b3[:, half - 1:half, :], b3.shape).reshape(cs, gw)
            sign_log2e = jnp.tile(sgn_ref[(j - 2) * cs:(j - 1) * cs, :], (1, HGRN_GROUP))
            level_e.append(jnp.exp2((b - mid) * sign_log2e))
        e_incl = jnp.exp(b)
        e_suffix = jnp.exp(b_last - b)

        for hh in range(HGRN_GROUP):
            head = grp * HGRN_GROUP + hh
            sl = slice(head * HEAD_DIM, (head + 1) * HEAD_DIM)
            loc = slice(hh * HEAD_DIM, (hh + 1) * HEAD_DIM)
            qb = q_ref[:, sl]
            qh = qb.astype(F32)
            kb = kk_ref[:, sl]
            kh = kb.astype(F32)
            vb = i_ref[:, sl]
            vh = vb.astype(F32)
            a_mat = jnp.zeros((cs, cs), F32)
            for j in range(HGRN_LEVELS):
                ej = level_e[j][:, loc].astype(BF16)
                aj = lax.dot_general(qb * ej, kb * ej, nt, preferred_element_type=F32)
                a_mat = jnp.where(level_mask[j], aj, a_mat)
            q_dec = qb * e_incl[:, loc].astype(BF16)
            k_dec = kb * e_suffix[:, loc].astype(BF16)
            st = st_ref[head]
            inter = lax.dot_general(q_dec, st.astype(BF16), nt, preferred_element_type=F32)
            intra = jnp.dot(a_mat.astype(BF16), vb, preferred_element_type=F32)
            diag = jnp.sum(qh * kh, axis=-1, keepdims=True) * vh
            o = inter + intra + diag
            st_ref[head] = (st * e_incl[cs - 1:cs, loc]
                            + lax.dot_general(vb, k_dec, tn, preferred_element_type=F32))
            gt = gt_ref[:, sl].astype(F32)
            on = _rms_scale(o, gn_ref[...]) * (gt * (1.0 / (1.0 + jnp.exp(-gt))))
            o_ref[:, sl] = on.astype(o_ref.dtype)


def _hgrn_recurrence(qig, log_f, kk, out_norm_gain, batch, seq, d_model):
    wb = HGRN_HEADS_PER_STEP * HEAD_DIM
    hb = d_model // wb
    tri, lvl, sgn = _hgrn_constants()

    def part(k):
        return pl.BlockSpec((None, None, HGRN_CHUNK, wb), lambda b, g, c: (k, b, c, g))

    rows = pl.BlockSpec((None, HGRN_CHUNK, wb), lambda b, g, c: (b, c, g))

    fixed = lambda b, g, c: (0, 0)
    out = pl.pallas_call(
        _hgrn_kernel,
        grid=(batch, hb, seq // HGRN_CHUNK),
        in_specs=[part(0), part(1), part(2), rows, rows,
                  pl.BlockSpec((1, HEAD_DIM), fixed),
                  pl.BlockSpec(tri.shape, fixed),
                  pl.BlockSpec(lvl.shape, fixed),
                  pl.BlockSpec(sgn.shape, fixed)],
        out_specs=rows,
        out_shape=jax.ShapeDtypeStruct((batch, seq, d_model), BF16),
        scratch_shapes=[pltpu.VMEM((HGRN_HEADS_PER_STEP, HEAD_DIM, HEAD_DIM), F32)],
        compiler_params=_params(("parallel", "parallel", "arbitrary")),
        name="hgrn_recurrence",
    )(qig, qig, qig, log_f, kk, out_norm_gain.reshape(1, HEAD_DIM),
      jnp.asarray(tri, BF16), jnp.asarray(lvl), jnp.asarray(sgn))
    return out.reshape(batch * seq, d_model)


@jax.jit
def kernel(x, positions, norm_gains, w_att_in, w_att_out, w_rec_in, rec_lower_bounds, rec_out_norm,
           w_rec_out, w_ff1, w_ff2):
    batch, seq, d_model = x.shape
    depth = norm_gains.shape[0]
    tables = [_rope_tables(positions, dilation) for _, dilation in DILATED_GROUPS]
    h = x.reshape(batch * seq, d_model)
    for layer in range(depth):
        g = norm_gains[layer]
        j = layer // 2
        if layer % 2 == 0:
            outs, lses = [], []
            for gi, (window, dilation) in enumerate(DILATED_GROUPS):
                assert window // dilation == ATT_BLOCK
                qkv = _att_in_proj(h, g[0], w_att_in, j, gi, dilation, batch, seq, tables[gi])
                o, lse = _attention(qkv, dilation, batch, seq)
                outs.append(o)
                lses.append(lse)
            h = _merge_out_proj(h, g[1], w_att_out, j, outs, lses, batch, seq)
        else:
            qig = _part_proj(h, g[0], w_rec_in, j, (0, 2, 3))
            log_f, kk = _forget_proj(h, g[0], w_rec_in, j, 1, rec_lower_bounds)
            mixed = _hgrn_recurrence(qig.reshape(3, batch, seq, d_model), log_f.reshape(batch, seq, d_model),
                                     kk.reshape(batch, seq, d_model), rec_out_norm[j], batch, seq, d_model)
            h = _out_proj(h, g[1], w_rec_out, j, mixed)
        h = _mlp(h, g[2], g[3], w_ff1, w_ff2, layer)
    return h.reshape(batch, seq, d_model)
```

```python
import functools

import numpy as np
import jax
import jax.numpy as jnp
from jax import lax
from jax.experimental import pallas as pl
from jax.experimental.pallas import tpu as pltpu

F32 = jnp.float32
BF16 = jnp.bfloat16

NORM_EPS = 1e-6
MASK_VALUE = -1e30
LB_FLOOR = 1e-30

DILATED_GROUPS = ((128, 1), (512, 4), (2048, 16))
HEAD_DIM = 128
ATT_HEADS = 8
ATT_BLOCK = 128
ATT_SUBS = 4
ATT_GROUP_WIDTH = ATT_HEADS * HEAD_DIM
ROPE_THETA = 500000.0
ROPE_DIM = HEAD_DIM // 4
ROPE_HALF = ROPE_DIM // 2
TOKEN_TILE = 256
PROJ_TILE_M, PROJ_SUB = 512, 256
OUT_TILE, OUT_SUB = 512, 256
MLP_TILE_M, MLP_TILE_F = 1024, 512
HGRN_CHUNK = 128
HGRN_LEVELS = 7
HGRN_HEADS_PER_STEP = 16
HGRN_GROUP = 16
LOG2E = 1.4426950408889634

VMEM_LIMIT_BYTES = 56 * 1024 * 1024
SINGLE = pl.Buffered(1)


def _params(semantics):
    return pltpu.CompilerParams(dimension_semantics=semantics, vmem_limit_bytes=VMEM_LIMIT_BYTES)


def _rms_scale(x, gain):
    ms = jnp.mean(x * x, axis=-1, keepdims=True)
    return x * lax.rsqrt(ms + NORM_EPS) * gain


def _rope_kernel(pos_ref, invf_ref, cos_ref, sa_ref, sb_ref):
    ang = pos_ref[...].astype(F32) * invf_ref[...]
    lane = lax.broadcasted_iota(jnp.int32, ang.shape, 1)
    c = jnp.cos(ang)
    s = jnp.sin(ang)
    cos_ref[...] = jnp.where(lane < ROPE_DIM, c, 1.0)
    sa_ref[...] = jnp.where((lane >= ROPE_HALF) & (lane < ROPE_DIM), s, 0.0)
    sb_ref[...] = jnp.where(lane < ROPE_HALF, -s, 0.0)


def _rope_tables(positions, dilation):
    batch, seq = positions.shape
    t = batch * seq
    tr = 1024
    pos = positions.reshape(batch, seq // dilation, dilation).transpose(0, 2, 1)
    inv_freq = ROPE_THETA ** (-jnp.arange(ROPE_HALF, dtype=F32) / ROPE_HALF)
    invf = jnp.zeros((1, HEAD_DIM), F32).at[0, :ROPE_DIM].set(jnp.tile(inv_freq, 2))
    out = jax.ShapeDtypeStruct((t, HEAD_DIM), F32)
    tabs = pl.pallas_call(
        _rope_kernel,
        grid=(t // tr,),
        in_specs=[pl.BlockSpec((tr, 1), lambda i: (i, 0)),
                  pl.BlockSpec((1, HEAD_DIM), lambda i: (0, 0))],
        out_specs=[pl.BlockSpec((tr, HEAD_DIM), lambda i: (i, 0))] * 3,
        out_shape=[out, out, out],
        compiler_params=_params(("parallel",)),
        name="rope_tables",
    )(pos.reshape(t, 1), invf)
    return [tab.reshape(batch, dilation, seq // dilation, HEAD_DIM) for tab in tabs]


def _att_in_proj_kernel(*refs, dilation):
    refs = list(refs)
    planes_ref = refs.pop() if dilation > 1 else None
    x_ref, g_ref, w_ref, cos_ref, sa_ref, sb_ref, o_ref, wb_ref, xn_ref = refs
    tm, d_model = x_ref.shape
    n = tm // dilation

    _cast_rows_once(w_ref, wb_ref, (pl.program_id(0) == 0) & (pl.program_id(1) == 0))
    xn = _rms_scale(x_ref[...], g_ref[...])
    if dilation == 1:
        xn_ref[...] = xn.astype(BF16)
    else:
        for c in range(d_model // HEAD_DIM):
            planes_ref[c] = xn[:, c * HEAD_DIM:(c + 1) * HEAD_DIM]
        for r in range(dilation):
            for c in range(d_model // HEAD_DIM):
                rows = planes_ref[c, pl.ds(r, n, stride=dilation), :]
                xn_ref[r * n:(r + 1) * n, c * HEAD_DIM:(c + 1) * HEAD_DIM] = rows.astype(BF16)

    acc = jnp.dot(xn_ref[...], wb_ref[...], preferred_element_type=F32)

    def store(lo, val):
        for r in range(dilation):
            o_ref[r, :, lo:lo + val.shape[1]] = val[r * n:(r + 1) * n].astype(o_ref.dtype)

    cos, sa, sb = (ref[...].reshape(tm, HEAD_DIM) for ref in (cos_ref, sa_ref, sb_ref))
    scale = HEAD_DIM ** -0.5
    for part, mult in ((0, scale), (1, 1.0)):
        c, a, b = cos * mult, sa * mult, sb * mult
        for hd in range(ATT_HEADS):
            lo = part * ATT_GROUP_WIDTH + hd * HEAD_DIM
            xh = acc[:, lo:lo + HEAD_DIM]
            rot = (xh * c + pltpu.roll(xh, ROPE_HALF, 1) * a
                   + pltpu.roll(xh, HEAD_DIM - ROPE_HALF, 1) * b)
            store(lo, rot)
    store(2 * ATT_GROUP_WIDTH, acc[:, 2 * ATT_GROUP_WIDTH:])


def _att_in_proj(h, gain, w, layer, group, dilation, batch, seq, tables):
    t, d_model = h.shape
    tm = TOKEN_TILE
    tiles = seq // tm
    n = tm // dilation
    width = 3 * ATT_GROUP_WIDTH
    residue_tile = lambda b, i: (b, 0, i, 0)
    scratch = [pltpu.VMEM((d_model, width), BF16), pltpu.VMEM((tm, d_model), BF16)]
    if dilation > 1:
        scratch.append(pltpu.VMEM((d_model // HEAD_DIM, tm, HEAD_DIM), F32))
    return pl.pallas_call(
        functools.partial(_att_in_proj_kernel, dilation=dilation),
        grid=(batch, tiles),
        in_specs=[pl.BlockSpec((tm, d_model), lambda b, i: (b * tiles + i, 0)),
                  pl.BlockSpec((1, d_model), lambda b, i: (0, 0)),
                  pl.BlockSpec((None, d_model, width), lambda b, i: (layer, 0, group), pipeline_mode=SINGLE)]
                 + [pl.BlockSpec((None, dilation, n, HEAD_DIM), residue_tile)] * 3,
        out_specs=pl.BlockSpec((None, dilation, n, width), residue_tile),
        out_shape=jax.ShapeDtypeStruct((batch, dilation, seq // dilation, width), BF16),
        scratch_shapes=scratch,
        compiler_params=_params(("arbitrary", "arbitrary")),
        name="att_in_proj",
    )(h, gain.reshape(1, d_model), w, *tables)


def _cast_rows_once(w_ref, wb_ref, first):
    @pl.when(first)
    def _():
        def cast_rows(i, carry):
            rows = pl.ds(pl.multiple_of(i * HEAD_DIM, HEAD_DIM), HEAD_DIM)
            wb_ref[rows, :] = w_ref[rows, :].astype(BF16)
            return carry
        lax.fori_loop(0, w_ref.shape[0] // HEAD_DIM, cast_rows, 0)


def _part_proj_kernel(x_ref, g_ref, w_ref, o_ref, wb_ref):
    _cast_rows_once(w_ref, wb_ref, pl.program_id(1) == 0)
    for lo in range(0, x_ref.shape[0], PROJ_SUB):
        rows = slice(lo, lo + PROJ_SUB)
        xn = _rms_scale(x_ref[rows, :], g_ref[...]).astype(BF16)
        o_ref[rows, :] = jnp.dot(xn, wb_ref[...], preferred_element_type=F32).astype(o_ref.dtype)


def _forget_proj_kernel(x_ref, g_ref, w_ref, lbp_ref, logf_ref, kk_ref, wb_ref, *, layer_j):
    _cast_rows_once(w_ref, wb_ref, pl.program_id(0) == 0)
    lbp = lbp_ref[...]
    e = jnp.exp(lbp - jnp.max(lbp, axis=0, keepdims=True))
    p = e / jnp.sum(e, axis=0, keepdims=True)
    csum = p[0:1]
    for t in range(1, layer_j + 1):
        csum = csum + p[t:t + 1]
    lb = csum - p[0:1]
    lb_floor = jnp.maximum(lb, LB_FLOOR)
    one_minus_lb = 1.0 - lb
    for lo in range(0, x_ref.shape[0], PROJ_SUB):
        rows = slice(lo, lo + PROJ_SUB)
        xn = _rms_scale(x_ref[rows, :], g_ref[...]).astype(BF16)
        z = jnp.dot(xn, wb_ref[...], preferred_element_type=F32)
        en = jnp.exp(-jnp.abs(z))
        rcp = 1.0 / (1.0 + en)
        pos = z >= 0.0
        logf_ref[rows, :] = jnp.log(lb_floor + one_minus_lb * (jnp.where(pos, 1.0, en) * rcp))
        kk_ref[rows, :] = (one_minus_lb * (jnp.where(pos, en, 1.0) * rcp) - (lb_floor - lb)).astype(kk_ref.dtype)


def _part_proj(h, gain, w, layer, parts):
    t, d_model = h.shape
    tm = PROJ_TILE_M

    def part_of(p):
        idx = parts[-1]
        for k in range(len(parts) - 2, -1, -1):
            idx = jnp.where(p == k, parts[k], idx)
        return idx

    return pl.pallas_call(
        _part_proj_kernel,
        grid=(len(parts), t // tm),
        in_specs=[pl.BlockSpec((tm, d_model), lambda p, i: (i, 0)),
                  pl.BlockSpec((1, d_model), lambda p, i: (0, 0)),
                  pl.BlockSpec((None, d_model, d_model), lambda p, i: (layer, 0, part_of(p)),
                               pipeline_mode=SINGLE)],
        out_specs=pl.BlockSpec((None, tm, d_model), lambda p, i: (p, i, 0)),
        out_shape=jax.ShapeDtypeStruct((len(parts), t, d_model), BF16),
        scratch_shapes=[pltpu.VMEM((d_model, d_model), BF16)],
        compiler_params=_params(("arbitrary", "arbitrary")),
        name="part_proj",
    )(h, gain.reshape(1, d_model), w)


def _forget_proj(h, gain, w, layer, part, lower_bound_params):
    t, d_model = h.shape
    tm = PROJ_TILE_M
    n_layers = lower_bound_params.shape[0]
    row = lambda i: (i, 0)
    fixed = lambda i: (0, 0)
    return pl.pallas_call(
        functools.partial(_forget_proj_kernel, layer_j=layer),
        grid=(t // tm,),
        in_specs=[pl.BlockSpec((tm, d_model), row),
                  pl.BlockSpec((1, d_model), fixed),
                  pl.BlockSpec((None, d_model, d_model), lambda i: (layer, 0, part), pipeline_mode=SINGLE),
                  pl.BlockSpec((n_layers, d_model), fixed)],
        out_specs=[pl.BlockSpec((tm, d_model), row), pl.BlockSpec((tm, d_model), row)],
        out_shape=[jax.ShapeDtypeStruct((t, d_model), F32), jax.ShapeDtypeStruct((t, d_model), BF16)],
        scratch_shapes=[pltpu.VMEM((d_model, d_model), BF16)],
        compiler_params=_params(("arbitrary",)),
        name="forget_proj",
    )(h, gain.reshape(1, d_model), w, lower_bound_params)


def _attention_kernel(*refs, has_prev):
    if has_prev:
        q_ref, k_ref, v_ref, o_ref, lse_ref, s_ref, prev_ref = refs
        first_block = pl.program_id(2) == 0

        @pl.when(first_block)
        def _():
            prev_ref[...] = jnp.zeros_like(prev_ref)
    else:
        q_ref, k_ref, v_ref, o_ref, lse_ref, s_ref = refs
    n_keys = s_ref.shape[2]
    row = lax.broadcasted_iota(jnp.int32, (ATT_BLOCK, n_keys), 0)
    col = lax.broadcasted_iota(jnp.int32, (ATT_BLOCK, n_keys), 1)
    if has_prev:
        band = (col >= row) & (col <= row + ATT_BLOCK)
    else:
        band = col <= row
    nt = (((1,), (1,)), ((), ()))
    heads = [slice(hd * HEAD_DIM, (hd + 1) * HEAD_DIM) for hd in range(ATT_HEADS)]
    lane = lax.broadcasted_iota(jnp.int32, (ATT_BLOCK, HEAD_DIM), 1)

    for sub in range(ATT_SUBS):
        rows = slice(sub * ATT_BLOCK, (sub + 1) * ATT_BLOCK)
        if has_prev:
            before = slice((sub - 1) * ATT_BLOCK, sub * ATT_BLOCK)
            mask = band & ((col >= ATT_BLOCK) | jnp.logical_not(first_block)) if sub == 0 else band

            def keys(sl):
                prev = prev_ref[0, :, sl] if sub == 0 else k_ref[before, sl]
                return jnp.concatenate([prev, k_ref[rows, sl]], axis=0)

            def values(sl):
                prev = prev_ref[1, :, sl] if sub == 0 else v_ref[before, sl]
                return jnp.concatenate([prev, v_ref[rows, sl]], axis=0)

            queries = lambda sl: q_ref[rows, sl]
            o_blk, lse_blk = o_ref.at[rows], lse_ref.at[rows]
        else:
            mask = band
            keys = lambda sl: k_ref[sub, :, sl]
            values = lambda sl: v_ref[sub, :, sl]
            queries = lambda sl: q_ref[sub, :, sl]
            o_blk, lse_blk = o_ref.at[sub], lse_ref.at[sub]

        maxes = []
        for hd, sl in enumerate(heads):
            s = lax.dot_general(queries(sl), keys(sl), nt, preferred_element_type=F32)
            s = jnp.where(mask, s, MASK_VALUE)
            s_ref[sub * ATT_HEADS + hd] = s
            maxes.append(jnp.max(s, axis=-1, keepdims=True))
        lse_tile = jnp.zeros((ATT_BLOCK, HEAD_DIM), F32)
        for hd, sl in enumerate(heads):
            p = jnp.exp(s_ref[sub * ATT_HEADS + hd] - maxes[hd])
            l = jnp.sum(p, axis=-1, keepdims=True)
            acc = jnp.dot(p.astype(BF16), values(sl), preferred_element_type=F32)
            o_blk[:, sl] = (acc / l).astype(o_ref.dtype)
            lse_tile = jnp.where(lane == hd, maxes[hd] + jnp.log(l), lse_tile)
        lse_blk[...] = lse_tile
    if has_prev:
        last = slice((ATT_SUBS - 1) * ATT_BLOCK, ATT_SUBS * ATT_BLOCK)
        prev_ref[0] = k_ref[last, :]
        prev_ref[1] = v_ref[last, :]


def _attention(qkv, dilation, batch, seq):
    length = seq // dilation
    nb = length // ATT_BLOCK
    has_prev = nb > 1
    if has_prev:
        grid = (batch, dilation, nb // ATT_SUBS)
        shape = lambda w: (None, None, ATT_SUBS * ATT_BLOCK, w)
    else:
        grid = (batch, dilation // ATT_SUBS, nb)
        shape = lambda w: (None, ATT_SUBS, ATT_BLOCK, w)

    def spec(w, part):
        return pl.BlockSpec(shape(w), lambda b, r, i: (b, r, i, part))

    n_keys = 2 * ATT_BLOCK if has_prev else ATT_BLOCK
    scratch = [pltpu.VMEM((ATT_SUBS * ATT_HEADS, ATT_BLOCK, n_keys), F32)]
    if has_prev:
        scratch.append(pltpu.VMEM((2, ATT_BLOCK, ATT_GROUP_WIDTH), BF16))
    return pl.pallas_call(
        functools.partial(_attention_kernel, has_prev=has_prev),
        grid=grid,
        in_specs=[spec(ATT_GROUP_WIDTH, 0), spec(ATT_GROUP_WIDTH, 1), spec(ATT_GROUP_WIDTH, 2)],
        out_specs=[spec(ATT_GROUP_WIDTH, 0), spec(HEAD_DIM, 0)],
        out_shape=[jax.ShapeDtypeStruct((batch, dilation, length, ATT_GROUP_WIDTH), BF16),
                   jax.ShapeDtypeStruct((batch, dilation, length, HEAD_DIM), F32)],
        scratch_shapes=scratch,
        compiler_params=_params(("parallel", "parallel", "arbitrary")),
        name="banded_attention",
    )(qkv, qkv, qkv)


def _cast_weight_once(w_ref, wb_ref, first):
    @pl.when(first)
    def _():
        wb_ref[...] = w_ref[...].astype(BF16)


def _project_norm_residual(a_ref, wb_ref, h_ref, g_ref, out_ref):
    for lo in range(0, a_ref.shape[0], OUT_SUB):
        rows = slice(lo, lo + OUT_SUB)
        y = jnp.dot(a_ref[rows, :], wb_ref[...], preferred_element_type=F32)
        out_ref[rows, :] = h_ref[rows, :] + _rms_scale(y, g_ref[...])


def _merge_out_proj_kernel(o1, o2, o3, l1, l2, l3, h_ref, g_ref, w_ref, out_ref, wb_ref, ot_ref, lt_ref, a_ref):
    _cast_weight_once(w_ref, wb_ref, (pl.program_id(0) == 0) & (pl.program_id(1) == 0))
    tm = h_ref.shape[0]
    for gi, (o_g, l_g) in enumerate(((o2, l2), (o3, l3))):
        dilation = o_g.shape[0]
        n = tm // dilation
        for r in range(dilation):
            rows = pl.ds(r, n, stride=dilation)
            lt_ref[gi, rows, :] = l_g[r]
            for hd in range(ATT_HEADS):
                ot_ref[gi, hd, rows, :] = o_g[r, :, hd * HEAD_DIM:(hd + 1) * HEAD_DIM].astype(F32)
    lses = (l1[0], lt_ref[0], lt_ref[1])
    m = jnp.maximum(jnp.maximum(lses[0], lses[1]), lses[2])
    es = [jnp.exp(v - m) for v in lses]
    den = es[0] + es[1] + es[2]
    ws = [e / den for e in es]
    for hd in range(ATT_HEADS):
        sl = slice(hd * HEAD_DIM, (hd + 1) * HEAD_DIM)
        acc = ws[0][:, hd:hd + 1] * o1[0, :, sl].astype(F32)
        acc = acc + ws[1][:, hd:hd + 1] * ot_ref[0, hd]
        acc = acc + ws[2][:, hd:hd + 1] * ot_ref[1, hd]
        a_ref[:, sl] = acc.astype(BF16)
    _project_norm_residual(a_ref, wb_ref, h_ref, g_ref, out_ref)


def _merge_out_proj(h, gain, w, layer, outs, lses, batch, seq):
    t, d_model = h.shape
    k = w.shape[1]
    tm = OUT_TILE
    tiles = seq // tm
    token_tile = lambda b, i: (b * tiles + i, 0)
    fixed = lambda b, i: (0, 0)

    def group_spec(arr):
        dilation, width = arr.shape[1], arr.shape[3]
        return pl.BlockSpec((None, dilation, tm // dilation, width), lambda b, i: (b, 0, i, 0))

    in_specs = [group_spec(a) for a in outs] + [group_spec(a) for a in lses]
    in_specs += [pl.BlockSpec((tm, d_model), token_tile), pl.BlockSpec((1, d_model), fixed),
                 pl.BlockSpec((None, k, d_model), lambda b, i: (layer, 0, 0), pipeline_mode=SINGLE)]
    return pl.pallas_call(
        _merge_out_proj_kernel,
        grid=(batch, tiles),
        in_specs=in_specs,
        out_specs=pl.BlockSpec((tm, d_model), token_tile),
        out_shape=jax.ShapeDtypeStruct((t, d_model), F32),
        scratch_shapes=[pltpu.VMEM((k, d_model), BF16),
                        pltpu.VMEM((2, ATT_HEADS, tm, HEAD_DIM), F32), pltpu.VMEM((2, tm, HEAD_DIM), F32),
                        pltpu.VMEM((tm, k), BF16)],
        compiler_params=_params(("arbitrary", "arbitrary")),
        name="merge_out_proj",
    )(*outs, *lses, h, gain.reshape(1, d_model), w)


def _out_proj_kernel(a_ref, h_ref, g_ref, w_ref, out_ref, wb_ref):
    _cast_weight_once(w_ref, wb_ref, pl.program_id(0) == 0)
    _project_norm_residual(a_ref, wb_ref, h_ref, g_ref, out_ref)


def _out_proj(h, gain, w, layer, mixed):
    t, d_model = h.shape
    k = w.shape[1]
    tm = OUT_TILE
    row = lambda i: (i, 0)
    fixed = lambda i: (0, 0)
    return pl.pallas_call(
        _out_proj_kernel,
        grid=(t // tm,),
        in_specs=[pl.BlockSpec((tm, k), row), pl.BlockSpec((tm, d_model), row),
                  pl.BlockSpec((1, d_model), fixed),
                  pl.BlockSpec((None, k, d_model), lambda i: (layer, 0, 0), pipeline_mode=SINGLE)],
        out_specs=pl.BlockSpec((tm, d_model), row),
        out_shape=jax.ShapeDtypeStruct((t, d_model), F32),
        scratch_shapes=[pltpu.VMEM((k, d_model), BF16)],
        compiler_params=_params(("arbitrary",)),
        name="out_proj",
    )(mixed, h, gain.reshape(1, d_model), w)


def _mlp_kernel(h_ref, g_in_ref, g_out_ref, w1_ref, w2_ref, out_ref, un_ref):
    f = pl.program_id(1)

    @pl.when(f == 0)
    def _():
        un_ref[...] = _rms_scale(h_ref[...], g_in_ref[...]).astype(BF16)
        out_ref[...] = jnp.zeros_like(out_ref)

    a = jnp.dot(un_ref[...], w1_ref[...].astype(BF16), preferred_element_type=F32)
    a = jnp.square(jnp.maximum(a, 0.0))
    out_ref[...] += jnp.dot(a.astype(BF16), w2_ref[...].astype(BF16), preferred_element_type=F32)

    @pl.when(f == pl.num_programs(1) - 1)
    def _():
        out_ref[...] = h_ref[...] + _rms_scale(out_ref[...], g_out_ref[...])


def _mlp(h, g_in, g_out, w1, w2, layer):
    t, d_model = h.shape
    d_ff = w1.shape[2]
    tm, tf = MLP_TILE_M, MLP_TILE_F
    return pl.pallas_call(
        _mlp_kernel,
        grid=(t // tm, d_ff // tf),
        in_specs=[pl.BlockSpec((tm, d_model), lambda m, f: (m, 0), pipeline_mode=SINGLE),
                  pl.BlockSpec((1, d_model), lambda m, f: (0, 0)),
                  pl.BlockSpec((1, d_model), lambda m, f: (0, 0)),
                  pl.BlockSpec((None, d_model, tf), lambda m, f: (layer, 0, f)),
                  pl.BlockSpec((None, tf, d_model), lambda m, f: (layer, f, 0))],
        out_specs=pl.BlockSpec((tm, d_model), lambda m, f: (m, 0)),
        out_shape=jax.ShapeDtypeStruct((t, d_model), F32),
        scratch_shapes=[pltpu.VMEM((tm, d_model), BF16)],
        compiler_params=_params(("parallel", "arbitrary")),
        name="mlp",
    )(h, g_in.reshape(1, d_model), g_out.reshape(1, d_model), w1, w2)


def _hgrn_constants():
    r = np.arange(HGRN_CHUNK)[:, None]
    c = np.arange(HGRN_CHUNK)[None, :]
    tri = (c <= r).astype(np.float32)
    msb = np.floor(np.log2(np.maximum(r ^ c, 1))).astype(np.int32)
    lvl = np.where(c < r, msb, -1).astype(np.int32)
    sgn = np.concatenate([np.where((r >> j) & 1 == 1, LOG2E, -LOG2E) * np.ones_like(c)
                          for j in range(2, HGRN_LEVELS)], axis=0).astype(np.float32)
    return tri, lvl, sgn


def _hgrn_kernel(q_ref, i_ref, gt_ref, logf_ref, kk_ref, gn_ref, tri_ref, lvl_ref, sgn_ref, o_ref, st_ref):
    nh = HGRN_HEADS_PER_STEP
    cs = HGRN_CHUNK

    @pl.when(pl.program_id(2) == 0)
    def _():
        st_ref[...] = jnp.zeros_like(st_ref)

    tri = tri_ref[...]
    lvl = lvl_ref[...]
    level_mask = [lvl == j for j in range(HGRN_LEVELS)]
    gw = HGRN_GROUP * HEAD_DIM
    row = lax.broadcasted_iota(jnp.int32, (cs, gw), 0)
    r4 = row & 3
    nt = (((1,), (1,)), ((), ()))
    tn = (((0,), (0,)), ((), ()))
    for grp in range(nh // HGRN_GROUP):
        gsl = slice(grp * gw, (grp + 1) * gw)
        log_f = logf_ref[:, gsl]

        g_hi = log_f.astype(BF16)
        r1 = log_f - g_hi.astype(F32)
        g_mid = r1.astype(BF16)
        g_lo = (r1 - g_mid.astype(F32)).astype(BF16)
        b = (jnp.dot(tri, g_hi, preferred_element_type=F32)
             + jnp.dot(tri, g_mid, preferred_element_type=F32)
             + jnp.dot(tri, g_lo, preferred_element_type=F32))
        b_last = b[cs - 1:cs, :]

        up = pltpu.roll(log_f, cs - 1, 0)
        down = pltpu.roll(log_f, 1, 0)
        level_e = [
            jnp.exp(jnp.where((row & 1) == 1, log_f, 0.0)),
            jnp.exp(jnp.where(r4 == 0, up, jnp.where(r4 == 1, 0.0, jnp.where(r4 == 2, log_f, log_f + down)))),
        ]
        for j in range(2, HGRN_LEVELS):
            half = 1 << j
            nblk = cs // (2 * half)
            b3 = b.reshape(nblk, 2 * half, gw)
            mid = jnp.broadcast_to(b3[:, half - 1:half, :], b3.shape).reshape(cs, gw)
            sign_log2e = jnp.tile(sgn_ref[(j - 2) * cs:(j - 1) * cs, :], (1, HGRN_GROUP))
            level_e.append(jnp.exp2((b - mid) * sign_log2e))
        e_incl = jnp.exp(b)
        e_suffix = jnp.exp(b_last - b)

        for hh in range(HGRN_GROUP):
            head = grp * HGRN_GROUP + hh
            sl = slice(head * HEAD_DIM, (head + 1) * HEAD_DIM)
            loc = slice(hh * HEAD_DIM, (hh + 1) * HEAD_DIM)
            qb = q_ref[:, sl]
            qh = qb.astype(F32)
            kb = kk_ref[:, sl]
            kh = kb.astype(F32)
            vb = i_ref[:, sl]
            vh = vb.astype(F32)
            a_mat = jnp.zeros((cs, cs), F32)
            for j in range(HGRN_LEVELS):
                ej = level_e[j][:, loc].astype(BF16)
                aj = lax.dot_general(qb * ej, kb * ej, nt, preferred_element_type=F32)
                a_mat = jnp.where(level_mask[j], aj, a_mat)
            q_dec = qb * e_incl[:, loc].astype(BF16)
            k_dec = kb * e_suffix[:, loc].astype(BF16)
            st = st_ref[head]
            inter = lax.dot_general(q_dec, st.astype(BF16), nt, preferred_element_type=F32)
            intra = jnp.dot(a_mat.astype(BF16), vb, preferred_element_type=F32)
            diag = jnp.sum(qh * kh, axis=-1, keepdims=True) * vh
            o = inter + intra + diag
            st_ref[head] = (st * e_incl[cs - 1:cs, loc]
                            + lax.dot_general(vb, k_dec, tn, preferred_element_type=F32))
            gt = gt_ref[:, sl].astype(F32)
            on = _rms_scale(o, gn_ref[...]) * (gt * (1.0 / (1.0 + jnp.exp(-gt))))
            o_ref[:, sl] = on.astype(o_ref.dtype)


def _hgrn_recurrence(qig, log_f, kk, out_norm_gain, batch, seq, d_model):
    wb = HGRN_HEADS_PER_STEP * HEAD_DIM
    hb = d_model // wb
    tri, lvl, sgn = _hgrn_constants()

    def part(k):
        return pl.BlockSpec((None, None, HGRN_CHUNK, wb), lambda b, g, c: (k, b, c, g))

    rows = pl.BlockSpec((None, HGRN_CHUNK, wb), lambda b, g, c: (b, c, g))

    fixed = lambda b, g, c: (0, 0)
    out = pl.pallas_call(
        _hgrn_kernel,
        grid=(batch, hb, seq // HGRN_CHUNK),
        in_specs=[part(0), part(1), part(2), rows, rows,
                  pl.BlockSpec((1, HEAD_DIM), fixed),
                  pl.BlockSpec(tri.shape, fixed),
                  pl.BlockSpec(lvl.shape, fixed),
                  pl.BlockSpec(sgn.shape, fixed)],
        out_specs=rows,
        out_shape=jax.ShapeDtypeStruct((batch, seq, d_model), BF16),
        scratch_shapes=[pltpu.VMEM((HGRN_HEADS_PER_STEP, HEAD_DIM, HEAD_DIM), F32)],
        compiler_params=_params(("parallel", "parallel", "arbitrary")),
        name="hgrn_recurrence",
    )(qig, qig, qig, log_f, kk, out_norm_gain.reshape(1, HEAD_DIM),
      jnp.asarray(tri, BF16), jnp.asarray(lvl), jnp.asarray(sgn))
    return out.reshape(batch * seq, d_model)


@jax.jit
def kernel(x, positions, norm_gains, w_att_in, w_att_out, w_rec_in, rec_lower_bounds, rec_out_norm,
           w_rec_out, w_ff1, w_ff2):
    batch, seq, d_model = x.shape
    depth = norm_gains.shape[0]
    tables = [_rope_tables(positions, dilation) for _, dilation in DILATED_GROUPS]
    h = x.reshape(batch * seq, d_model)
    for layer in range(depth):
        g = norm_gains[layer]
        j = layer // 2
        if layer % 2 == 0:
            outs, lses = [], []
            for gi, (window, dilation) in enumerate(DILATED_GROUPS):
                assert window // dilation == ATT_BLOCK
                qkv = _att_in_proj(h, g[0], w_att_in, j, gi, dilation, batch, seq, tables[gi])
                o, lse = _attention(qkv, dilation, batch, seq)
                outs.append(o)
                lses.append(lse)
            h = _merge_out_proj(h, g[1], w_att_out, j, outs, lses, batch, seq)
        else:
            qig = _part_proj(h, g[0], w_rec_in, j, (0, 2, 3))
            log_f, kk = _forget_proj(h, g[0], w_rec_in, j, 1, rec_lower_bounds)
            mixed = _hgrn_recurrence(qig.reshape(3, batch, seq, d_model), log_f.reshape(batch, seq, d_model),
                                     kk.reshape(batch, seq, d_model), rec_out_norm[j], batch, seq, d_model)
            h = _out_proj(h, g[1], w_rec_out, j, mixed)
        h = _mlp(h, g[2], g[3], w_ff1, w_ff2, layer)
    return h.reshape(batch, seq, d_model)
```

```python
import functools

import numpy as np
import jax
import jax.numpy as jnp
from jax import lax
from jax.experimental import pallas as pl
from jax.experimental.pallas import tpu as pltpu

F32 = jnp.float32
BF16 = jnp.bfloat16

NORM_EPS = 1e-6
MASK_VALUE = -1e30
LB_FLOOR = 1e-30

DILATED_GROUPS = ((128, 1), (512, 4), (2048, 16))
HEAD_DIM = 128
ATT_HEADS = 8
ATT_BLOCK = 128
ATT_SUBS = 4
ATT_GROUP_WIDTH = ATT_HEADS * HEAD_DIM
ROPE_THETA = 500000.0
ROPE_DIM = HEAD_DIM // 4
ROPE_HALF = ROPE_DIM // 2
TOKEN_TILE = 256
PROJ_TILE_M, PROJ_SUB = 512, 256
OUT_TILE, OUT_SUB = 512, 256
MLP_TILE_M, MLP_TILE_F = 1024, 512
HGRN_CHUNK = 128
HGRN_CHUNKS_PER_STEP = 2
HGRN_LEVELS = 7
HGRN_HEADS_PER_STEP = 16
HGRN_GROUP = 16
LOG2E = 1.4426950408889634

VMEM_LIMIT_BYTES = 56 * 1024 * 1024
SINGLE = pl.Buffered(1)


def _params(semantics):
    return pltpu.CompilerParams(dimension_semantics=semantics, vmem_limit_bytes=VMEM_LIMIT_BYTES)


def _rms_scale(x, gain):
    ms = jnp.mean(x * x, axis=-1, keepdims=True)
    return x * lax.rsqrt(ms + NORM_EPS) * gain


def _rope_kernel(pos_ref, invf_ref, cos_ref, sa_ref, sb_ref):
    ang = pos_ref[...].astype(F32) * invf_ref[...]
    lane = lax.broadcasted_iota(jnp.int32, ang.shape, 1)
    c = jnp.cos(ang)
    s = jnp.sin(ang)
    cos_ref[...] = jnp.where(lane < ROPE_DIM, c, 1.0)
    sa_ref[...] = jnp.where((lane >= ROPE_HALF) & (lane < ROPE_DIM), s, 0.0)
    sb_ref[...] = jnp.where(lane < ROPE_HALF, -s, 0.0)


def _rope_tables(positions, dilation):
    batch, seq = positions.shape
    t = batch * seq
    tr = 1024
    pos = positions.reshape(batch, seq // dilation, dilation).transpose(0, 2, 1)
    inv_freq = ROPE_THETA ** (-jnp.arange(ROPE_HALF, dtype=F32) / ROPE_HALF)
    invf = jnp.zeros((1, HEAD_DIM), F32).at[0, :ROPE_DIM].set(jnp.tile(inv_freq, 2))
    out = jax.ShapeDtypeStruct((t, HEAD_DIM), F32)
    tabs = pl.pallas_call(
        _rope_kernel,
        grid=(t // tr,),
        in_specs=[pl.BlockSpec((tr, 1), lambda i: (i, 0)),
                  pl.BlockSpec((1, HEAD_DIM), lambda i: (0, 0))],
        out_specs=[pl.BlockSpec((tr, HEAD_DIM), lambda i: (i, 0))] * 3,
        out_shape=[out, out, out],
        compiler_params=_params(("parallel",)),
        name="rope_tables",
    )(pos.reshape(t, 1), invf)
    return [tab.reshape(batch, dilation, seq // dilation, HEAD_DIM) for tab in tabs]


def _att_in_proj_kernel(*refs, dilation):
    refs = list(refs)
    planes_ref = refs.pop() if dilation > 1 else None
    x_ref, g_ref, w_ref, cos_ref, sa_ref, sb_ref, o_ref, wb_ref, xn_ref = refs
    tm, d_model = x_ref.shape
    n = tm // dilation

    _cast_rows_once(w_ref, wb_ref, (pl.program_id(0) == 0) & (pl.program_id(1) == 0))
    xn = _rms_scale(x_ref[...], g_ref[...])
    if dilation == 1:
        xn_ref[...] = xn.astype(BF16)
    else:
        for c in range(d_model // HEAD_DIM):
            planes_ref[c] = xn[:, c * HEAD_DIM:(c + 1) * HEAD_DIM]
        for r in range(dilation):
            for c in range(d_model // HEAD_DIM):
                rows = planes_ref[c, pl.ds(r, n, stride=dilation), :]
                xn_ref[r * n:(r + 1) * n, c * HEAD_DIM:(c + 1) * HEAD_DIM] = rows.astype(BF16)

    acc = jnp.dot(xn_ref[...], wb_ref[...], preferred_element_type=F32)

    def store(lo, val):
        for r in range(dilation):
            o_ref[r, :, lo:lo + val.shape[1]] = val[r * n:(r + 1) * n].astype(o_ref.dtype)

    cos, sa, sb = (ref[...].reshape(tm, HEAD_DIM) for ref in (cos_ref, sa_ref, sb_ref))
    scale = HEAD_DIM ** -0.5
    for part, mult in ((0, scale), (1, 1.0)):
        c, a, b = cos * mult, sa * mult, sb * mult
        for hd in range(ATT_HEADS):
            lo = part * ATT_GROUP_WIDTH + hd * HEAD_DIM
            xh = acc[:, lo:lo + HEAD_DIM]
            rot = (xh * c + pltpu.roll(xh, ROPE_HALF, 1) * a
                   + pltpu.roll(xh, HEAD_DIM - ROPE_HALF, 1) * b)
            store(lo, rot)
    store(2 * ATT_GROUP_WIDTH, acc[:, 2 * ATT_GROUP_WIDTH:])


def _att_in_proj(h, gain, w, layer, group, dilation, batch, seq, tables):
    t, d_model = h.shape
    tm = TOKEN_TILE
    tiles = seq // tm
    n = tm // dilation
    width = 3 * ATT_GROUP_WIDTH
    residue_tile = lambda b, i: (b, 0, i, 0)
    scratch = [pltpu.VMEM((d_model, width), BF16), pltpu.VMEM((tm, d_model), BF16)]
    if dilation > 1:
        scratch.append(pltpu.VMEM((d_model // HEAD_DIM, tm, HEAD_DIM), F32))
    return pl.pallas_call(
        functools.partial(_att_in_proj_kernel, dilation=dilation),
        grid=(batch, tiles),
        in_specs=[pl.BlockSpec((tm, d_model), lambda b, i: (b * tiles + i, 0)),
                  pl.BlockSpec((1, d_model), lambda b, i: (0, 0)),
                  pl.BlockSpec((None, d_model, width), lambda b, i: (layer, 0, group), pipeline_mode=SINGLE)]
                 + [pl.BlockSpec((None, dilation, n, HEAD_DIM), residue_tile)] * 3,
        out_specs=pl.BlockSpec((None, dilation, n, width), residue_tile),
        out_shape=jax.ShapeDtypeStruct((batch, dilation, seq // dilation, width), BF16),
        scratch_shapes=scratch,
        compiler_params=_params(("arbitrary", "arbitrary")),
        name="att_in_proj",
    )(h, gain.reshape(1, d_model), w, *tables)


def _cast_rows_once(w_ref, wb_ref, first):
    @pl.when(first)
    def _():
        def cast_rows(i, carry):
            rows = pl.ds(pl.multiple_of(i * HEAD_DIM, HEAD_DIM), HEAD_DIM)
            wb_ref[rows, :] = w_ref[rows, :].astype(BF16)
            return carry
        lax.fori_loop(0, w_ref.shape[0] // HEAD_DIM, cast_rows, 0)


def _part_proj_kernel(x_ref, g_ref, w_ref, o_ref, wb_ref):
    _cast_rows_once(w_ref, wb_ref, pl.program_id(1) == 0)
    for lo in range(0, x_ref.shape[0], PROJ_SUB):
        rows = slice(lo, lo + PROJ_SUB)
        xn = _rms_scale(x_ref[rows, :], g_ref[...]).astype(BF16)
        o_ref[rows, :] = jnp.dot(xn, wb_ref[...], preferred_element_type=F32).astype(o_ref.dtype)


def _forget_proj_kernel(x_ref, g_ref, w_ref, lbp_ref, logf_ref, kk_ref, wb_ref, *, layer_j):
    _cast_rows_once(w_ref, wb_ref, pl.program_id(0) == 0)
    lbp = lbp_ref[...]
    e = jnp.exp(lbp - jnp.max(lbp, axis=0, keepdims=True))
    p = e / jnp.sum(e, axis=0, keepdims=True)
    csum = p[0:1]
    for t in range(1, layer_j + 1):
        csum = csum + p[t:t + 1]
    lb = csum - p[0:1]
    lb_floor = jnp.maximum(lb, LB_FLOOR)
    one_minus_lb = 1.0 - lb
    for lo in range(0, x_ref.shape[0], PROJ_SUB):
        rows = slice(lo, lo + PROJ_SUB)
        xn = _rms_scale(x_ref[rows, :], g_ref[...]).astype(BF16)
        z = jnp.dot(xn, wb_ref[...], preferred_element_type=F32)
        en = jnp.exp(-jnp.abs(z))
        rcp = 1.0 / (1.0 + en)
        pos = z >= 0.0
        logf_ref[rows, :] = jnp.log(lb_floor + one_minus_lb * (jnp.where(pos, 1.0, en) * rcp))
        kk_ref[rows, :] = (one_minus_lb * (jnp.where(pos, en, 1.0) * rcp) - (lb_floor - lb)).astype(kk_ref.dtype)


def _part_proj(h, gain, w, layer, parts):
    t, d_model = h.shape
    tm = PROJ_TILE_M

    def part_of(p):
        idx = parts[-1]
        for k in range(len(parts) - 2, -1, -1):
            idx = jnp.where(p == k, parts[k], idx)
        return idx

    return pl.pallas_call(
        _part_proj_kernel,
        grid=(len(parts), t // tm),
        in_specs=[pl.BlockSpec((tm, d_model), lambda p, i: (i, 0)),
                  pl.BlockSpec((1, d_model), lambda p, i: (0, 0)),
                  pl.BlockSpec((None, d_model, d_model), lambda p, i: (layer, 0, part_of(p)),
                               pipeline_mode=SINGLE)],
        out_specs=pl.BlockSpec((None, tm, d_model), lambda p, i: (p, i, 0)),
        out_shape=jax.ShapeDtypeStruct((len(parts), t, d_model), BF16),
        scratch_shapes=[pltpu.VMEM((d_model, d_model), BF16)],
        compiler_params=_params(("arbitrary", "arbitrary")),
        name="part_proj",
    )(h, gain.reshape(1, d_model), w)


def _forget_proj(h, gain, w, layer, part, lower_bound_params):
    t, d_model = h.shape
    tm = PROJ_TILE_M
    n_layers = lower_bound_params.shape[0]
    row = lambda i: (i, 0)
    fixed = lambda i: (0, 0)
    return pl.pallas_call(
        functools.partial(_forget_proj_kernel, layer_j=layer),
        grid=(t // tm,),
        in_specs=[pl.BlockSpec((tm, d_model), row),
                  pl.BlockSpec((1, d_model), fixed),
                  pl.BlockSpec((None, d_model, d_model), lambda i: (layer, 0, part), pipeline_mode=SINGLE),
                  pl.BlockSpec((n_layers, d_model), fixed)],
        out_specs=[pl.BlockSpec((tm, d_model), row), pl.BlockSpec((tm, d_model), row)],
        out_shape=[jax.ShapeDtypeStruct((t, d_model), F32), jax.ShapeDtypeStruct((t, d_model), BF16)],
        scratch_shapes=[pltpu.VMEM((d_model, d_model), BF16)],
        compiler_params=_params(("arbitrary",)),
        name="forget_proj",
    )(h, gain.reshape(1, d_model), w, lower_bound_params)


def _attention_kernel(*refs, has_prev):
    if has_prev:
        q_ref, k_ref, v_ref, o_ref, lse_ref, s_ref, prev_ref = refs
        first_block = pl.program_id(2) == 0

        @pl.when(first_block)
        def _():
            prev_ref[...] = jnp.zeros_like(prev_ref)
    else:
        q_ref, k_ref, v_ref, o_ref, lse_ref, s_ref = refs
    n_keys = s_ref.shape[2]
    row = lax.broadcasted_iota(jnp.int32, (ATT_BLOCK, n_keys), 0)
    col = lax.broadcasted_iota(jnp.int32, (ATT_BLOCK, n_keys), 1)
    if has_prev:
        band = (col >= row) & (col <= row + ATT_BLOCK)
    else:
        band = col <= row
    nt = (((1,), (1,)), ((), ()))
    heads = [slice(hd * HEAD_DIM, (hd + 1) * HEAD_DIM) for hd in range(ATT_HEADS)]
    lane = lax.broadcasted_iota(jnp.int32, (ATT_BLOCK, HEAD_DIM), 1)

    for sub in range(ATT_SUBS):
        rows = slice(sub * ATT_BLOCK, (sub + 1) * ATT_BLOCK)
        if has_prev:
            before = slice((sub - 1) * ATT_BLOCK, sub * ATT_BLOCK)
            mask = band & ((col >= ATT_BLOCK) | jnp.logical_not(first_block)) if sub == 0 else band

            def keys(sl):
                prev = prev_ref[0, :, sl] if sub == 0 else k_ref[before, sl]
                return jnp.concatenate([prev, k_ref[rows, sl]], axis=0)

            def values(sl):
                prev = prev_ref[1, :, sl] if sub == 0 else v_ref[before, sl]
                return jnp.concatenate([prev, v_ref[rows, sl]], axis=0)

            queries = lambda sl: q_ref[rows, sl]
            o_blk, lse_blk = o_ref.at[rows], lse_ref.at[rows]
        else:
            mask = band
            keys = lambda sl: k_ref[sub, :, sl]
            values = lambda sl: v_ref[sub, :, sl]
            queries = lambda sl: q_ref[sub, :, sl]
            o_blk, lse_blk = o_ref.at[sub], lse_ref.at[sub]

        maxes = []
        for hd, sl in enumerate(heads):
            s = lax.dot_general(queries(sl), keys(sl), nt, preferred_element_type=F32)
            s = jnp.where(mask, s, MASK_VALUE)
            s_ref[sub * ATT_HEADS + hd] = s
            maxes.append(jnp.max(s, axis=-1, keepdims=True))
        lse_tile = jnp.zeros((ATT_BLOCK, HEAD_DIM), F32)
        for hd, sl in enumerate(heads):
            p = jnp.exp(s_ref[sub * ATT_HEADS + hd] - maxes[hd])
            l = jnp.sum(p, axis=-1, keepdims=True)
            acc = jnp.dot(p.astype(BF16), values(sl), preferred_element_type=F32)
            o_blk[:, sl] = (acc / l).astype(o_ref.dtype)
            lse_tile = jnp.where(lane == hd, maxes[hd] + jnp.log(l), lse_tile)
        lse_blk[...] = lse_tile
    if has_prev:
        last = slice((ATT_SUBS - 1) * ATT_BLOCK, ATT_SUBS * ATT_BLOCK)
        prev_ref[0] = k_ref[last, :]
        prev_ref[1] = v_ref[last, :]


def _attention(qkv, dilation, batch, seq):
    length = seq // dilation
    nb = length // ATT_BLOCK
    has_prev = nb > 1
    if has_prev:
        grid = (batch, dilation, nb // ATT_SUBS)
        shape = lambda w: (None, None, ATT_SUBS * ATT_BLOCK, w)
    else:
        grid = (batch, dilation // ATT_SUBS, nb)
        shape = lambda w: (None, ATT_SUBS, ATT_BLOCK, w)

    def spec(w, part):
        return pl.BlockSpec(shape(w), lambda b, r, i: (b, r, i, part))

    n_keys = 2 * ATT_BLOCK if has_prev else ATT_BLOCK
    scratch = [pltpu.VMEM((ATT_SUBS * ATT_HEADS, ATT_BLOCK, n_keys), F32)]
    if has_prev:
        scratch.append(pltpu.VMEM((2, ATT_BLOCK, ATT_GROUP_WIDTH), BF16))
    return pl.pallas_call(
        functools.partial(_attention_kernel, has_prev=has_prev),
        grid=grid,
        in_specs=[spec(ATT_GROUP_WIDTH, 0), spec(ATT_GROUP_WIDTH, 1), spec(ATT_GROUP_WIDTH, 2)],
        out_specs=[spec(ATT_GROUP_WIDTH, 0), spec(HEAD_DIM, 0)],
        out_shape=[jax.ShapeDtypeStruct((batch, dilation, length, ATT_GROUP_WIDTH), BF16),
                   jax.ShapeDtypeStruct((batch, dilation, length, HEAD_DIM), F32)],
        scratch_shapes=scratch,
        compiler_params=_params(("parallel", "parallel", "arbitrary")),
        name="banded_attention",
    )(qkv, qkv, qkv)


def _cast_weight_once(w_ref, wb_ref, first):
    @pl.when(first)
    def _():
        wb_ref[...] = w_ref[...].astype(BF16)


def _project_norm_residual(a_ref, wb_ref, h_ref, g_ref, out_ref):
    for lo in range(0, a_ref.shape[0], OUT_SUB):
        rows = slice(lo, lo + OUT_SUB)
        y = jnp.dot(a_ref[rows, :], wb_ref[...], preferred_element_type=F32)
        out_ref[rows, :] = h_ref[rows, :] + _rms_scale(y, g_ref[...])


def _merge_out_proj_kernel(o1, o2, o3, l1, l2, l3, h_ref, g_ref, w_ref, out_ref, wb_ref, ot_ref, lt_ref, a_ref):
    _cast_weight_once(w_ref, wb_ref, (pl.program_id(0) == 0) & (pl.program_id(1) == 0))
    tm = h_ref.shape[0]
    for gi, (o_g, l_g) in enumerate(((o2, l2), (o3, l3))):
        dilation = o_g.shape[0]
        n = tm // dilation
        for r in range(dilation):
            rows = pl.ds(r, n, stride=dilation)
            lt_ref[gi, rows, :] = l_g[r]
            for hd in range(ATT_HEADS):
                ot_ref[gi, hd, rows, :] = o_g[r, :, hd * HEAD_DIM:(hd + 1) * HEAD_DIM].astype(F32)
    lses = (l1[0], lt_ref[0], lt_ref[1])
    m = jnp.maximum(jnp.maximum(lses[0], lses[1]), lses[2])
    es = [jnp.exp(v - m) for v in lses]
    den = es[0] + es[1] + es[2]
    ws = [e / den for e in es]
    for hd in range(ATT_HEADS):
        sl = slice(hd * HEAD_DIM, (hd + 1) * HEAD_DIM)
        acc = ws[0][:, hd:hd + 1] * o1[0, :, sl].astype(F32)
        acc = acc + ws[1][:, hd:hd + 1] * ot_ref[0, hd]
        acc = acc + ws[2][:, hd:hd + 1] * ot_ref[1, hd]
        a_ref[:, sl] = acc.astype(BF16)
    _project_norm_residual(a_ref, wb_ref, h_ref, g_ref, out_ref)


def _merge_out_proj(h, gain, w, layer, outs, lses, batch, seq):
    t, d_model = h.shape
    k = w.shape[1]
    tm = OUT_TILE
    tiles = seq // tm
    token_tile = lambda b, i: (b * tiles + i, 0)
    fixed = lambda b, i: (0, 0)

    def group_spec(arr):
        dilation, width = arr.shape[1], arr.shape[3]
        return pl.BlockSpec((None, dilation, tm // dilation, width), lambda b, i: (b, 0, i, 0))

    in_specs = [group_spec(a) for a in outs] + [group_spec(a) for a in lses]
    in_specs += [pl.BlockSpec((tm, d_model), token_tile), pl.BlockSpec((1, d_model), fixed),
                 pl.BlockSpec((None, k, d_model), lambda b, i: (layer, 0, 0), pipeline_mode=SINGLE)]
    return pl.pallas_call(
        _merge_out_proj_kernel,
        grid=(batch, tiles),
        in_specs=in_specs,
        out_specs=pl.BlockSpec((tm, d_model), token_tile),
        out_shape=jax.ShapeDtypeStruct((t, d_model), F32),
        scratch_shapes=[pltpu.VMEM((k, d_model), BF16),
                        pltpu.VMEM((2, ATT_HEADS, tm, HEAD_DIM), F32), pltpu.VMEM((2, tm, HEAD_DIM), F32),
                        pltpu.VMEM((tm, k), BF16)],
        compiler_params=_params(("arbitrary", "arbitrary")),
        name="merge_out_proj",
    )(*outs, *lses, h, gain.reshape(1, d_model), w)


def _out_proj_kernel(a_ref, h_ref, g_ref, w_ref, out_ref, wb_ref):
    _cast_weight_once(w_ref, wb_ref, pl.program_id(0) == 0)
    _project_norm_residual(a_ref, wb_ref, h_ref, g_ref, out_ref)


def _out_proj(h, gain, w, layer, mixed):
    t, d_model = h.shape
    k = w.shape[1]
    tm = OUT_TILE
    row = lambda i: (i, 0)
    fixed = lambda i: (0, 0)
    return pl.pallas_call(
        _out_proj_kernel,
        grid=(t // tm,),
        in_specs=[pl.BlockSpec((tm, k), row), pl.BlockSpec((tm, d_model), row),
                  pl.BlockSpec((1, d_model), fixed),
                  pl.BlockSpec((None, k, d_model), lambda i: (layer, 0, 0), pipeline_mode=SINGLE)],
        out_specs=pl.BlockSpec((tm, d_model), row),
        out_shape=jax.ShapeDtypeStruct((t, d_model), F32),
        scratch_shapes=[pltpu.VMEM((k, d_model), BF16)],
        compiler_params=_params(("arbitrary",)),
        name="out_proj",
    )(mixed, h, gain.reshape(1, d_model), w)


def _mlp_kernel(h_ref, g_in_ref, g_out_ref, w1_ref, w2_ref, out_ref, un_ref):
    f = pl.program_id(1)

    @pl.when(f == 0)
    def _():
        un_ref[...] = _rms_scale(h_ref[...], g_in_ref[...]).astype(BF16)
        out_ref[...] = jnp.zeros_like(out_ref)

    a = jnp.dot(un_ref[...], w1_ref[...].astype(BF16), preferred_element_type=F32)
    a = jnp.square(jnp.maximum(a, 0.0))
    out_ref[...] += jnp.dot(a.astype(BF16), w2_ref[...].astype(BF16), preferred_element_type=F32)

    @pl.when(f == pl.num_programs(1) - 1)
    def _():
        out_ref[...] = h_ref[...] + _rms_scale(out_ref[...], g_out_ref[...])


def _mlp(h, g_in, g_out, w1, w2, layer):
    t, d_model = h.shape
    d_ff = w1.shape[2]
    tm, tf = MLP_TILE_M, MLP_TILE_F
    return pl.pallas_call(
        _mlp_kernel,
        grid=(t // tm, d_ff // tf),
        in_specs=[pl.BlockSpec((tm, d_model), lambda m, f: (m, 0), pipeline_mode=SINGLE),
                  pl.BlockSpec((1, d_model), lambda m, f: (0, 0)),
                  pl.BlockSpec((1, d_model), lambda m, f: (0, 0)),
                  pl.BlockSpec((None, d_model, tf), lambda m, f: (layer, 0, f)),
                  pl.BlockSpec((None, tf, d_model), lambda m, f: (layer, f, 0))],
        out_specs=pl.BlockSpec((tm, d_model), lambda m, f: (m, 0)),
        out_shape=jax.ShapeDtypeStruct((t, d_model), F32),
        scratch_shapes=[pltpu.VMEM((tm, d_model), BF16)],
        compiler_params=_params(("parallel", "arbitrary")),
        name="mlp",
    )(h, g_in.reshape(1, d_model), g_out.reshape(1, d_model), w1, w2)


def _hgrn_constants():
    r = np.arange(HGRN_CHUNK)[:, None]
    c = np.arange(HGRN_CHUNK)[None, :]
    tri = (c <= r).astype(np.float32)
    msb = np.floor(np.log2(np.maximum(r ^ c, 1))).astype(np.int32)
    lvl = np.where(c < r, msb, -1).astype(np.int32)
    sgn = np.concatenate([np.where((r >> j) & 1 == 1, LOG2E, -LOG2E) * np.ones_like(c)
                          for j in range(2, HGRN_LEVELS)], axis=0).astype(np.float32)
    return tri, lvl, sgn


def _hgrn_kernel(*refs):
    st_ref = refs[-1]

    @pl.when(pl.program_id(2) == 0)
    def _():
        st_ref[...] = jnp.zeros_like(st_ref)

    for ck in range(HGRN_CHUNKS_PER_STEP):
        _hgrn_chunk(slice(ck * HGRN_CHUNK, (ck + 1) * HGRN_CHUNK), *refs)


def _hgrn_chunk(rws, q_ref, i_ref, gt_ref, logf_ref, kk_ref, gn_ref, tri_ref, lvl_ref, sgn_ref, o_ref, st_ref):
    nh = HGRN_HEADS_PER_STEP
    cs = HGRN_CHUNK
    tri = tri_ref[...]
    lvl = lvl_ref[...]
    level_mask = [lvl == j for j in range(HGRN_LEVELS)]
    gw = HGRN_GROUP * HEAD_DIM
    row = lax.broadcasted_iota(jnp.int32, (cs, gw), 0)
    r4 = row & 3
    nt = (((1,), (1,)), ((), ()))
    tn = (((0,), (0,)), ((), ()))
    for grp in range(nh // HGRN_GROUP):
        gsl = slice(grp * gw, (grp + 1) * gw)
        log_f = logf_ref[rws, gsl]

        g_hi = log_f.astype(BF16)
        r1 = log_f - g_hi.astype(F32)
        g_mid = r1.astype(BF16)
        g_lo = (r1 - g_mid.astype(F32)).astype(BF16)
        b = (jnp.dot(tri, g_hi, preferred_element_type=F32)
             + jnp.dot(tri, g_mid, preferred_element_type=F32)
             + jnp.dot(tri, g_lo, preferred_element_type=F32))
        b_last = b[cs - 1:cs, :]

        up = pltpu.roll(log_f, cs - 1, 0)
        down = pltpu.roll(log_f, 1, 0)
        level_e = [
            jnp.exp(jnp.where((row & 1) == 1, log_f, 0.0)),
            jnp.exp(jnp.where(r4 == 0, up, jnp.where(r4 == 1, 0.0, jnp.where(r4 == 2, log_f, log_f + down)))),
        ]
        for j in range(2, HGRN_LEVELS):
            half = 1 << j
            nblk = cs // (2 * half)
            b3 = b.reshape(nblk, 2 * half, gw)
            mid = jnp.broadcast_to(b3[:, half - 1:half, :], b3.shape).reshape(cs, gw)
            sign_log2e = jnp.tile(sgn_ref[(j - 2) * cs:(j - 1) * cs, :], (1, HGRN_GROUP))
            level_e.append(jnp.exp2((b - mid) * sign_log2e))
        e_incl = jnp.exp(b)
        e_suffix = jnp.exp(b_last - b)

        for hh in range(HGRN_GROUP):
            head = grp * HGRN_GROUP + hh
            sl = slice(head * HEAD_DIM, (head + 1) * HEAD_DIM)
            loc = slice(hh * HEAD_DIM, (hh + 1) * HEAD_DIM)
            qb = q_ref[rws, sl]
            qh = qb.astype(F32)
            kb = kk_ref[rws, sl]
            kh = kb.astype(F32)
            vb = i_ref[rws, sl]
            vh = vb.astype(F32)
            a_mat = jnp.zeros((cs, cs), F32)
            for j in range(HGRN_LEVELS):
                ej = level_e[j][:, loc].astype(BF16)
                aj = lax.dot_general(qb * ej, kb * ej, nt, preferred_element_type=F32)
                a_mat = jnp.where(level_mask[j], aj, a_mat)
            q_dec = qb * e_incl[:, loc].astype(BF16)
            k_dec = kb * e_suffix[:, loc].astype(BF16)
            st = st_ref[head]
            inter = lax.dot_general(q_dec, st.astype(BF16), nt, preferred_element_type=F32)
            intra = jnp.dot(a_mat.astype(BF16), vb, preferred_element_type=F32)
            diag = jnp.sum(qh * kh, axis=-1, keepdims=True) * vh
            o = inter + intra + diag
            st_ref[head] = (st * e_incl[cs - 1:cs, loc]
                            + lax.dot_general(vb, k_dec, tn, preferred_element_type=F32))
            gt = gt_ref[rws, sl].astype(F32)
            on = _rms_scale(o, gn_ref[...]) * (gt * (1.0 / (1.0 + jnp.exp(-gt))))
            o_ref[rws, sl] = on.astype(o_ref.dtype)


def _hgrn_recurrence(qig, log_f, kk, out_norm_gain, batch, seq, d_model):
    wb = HGRN_HEADS_PER_STEP * HEAD_DIM
    hb = d_model // wb
    tri, lvl, sgn = _hgrn_constants()

    step_rows = HGRN_CHUNKS_PER_STEP * HGRN_CHUNK

    def part(k):
        return pl.BlockSpec((None, None, step_rows, wb), lambda b, g, c: (k, b, c, g))

    rows = pl.BlockSpec((None, step_rows, wb), lambda b, g, c: (b, c, g))

    fixed = lambda b, g, c: (0, 0)
    out = pl.pallas_call(
        _hgrn_kernel,
        grid=(batch, hb, seq // step_rows),
        in_specs=[part(0), part(1), part(2), rows, rows,
                  pl.BlockSpec((1, HEAD_DIM), fixed),
                  pl.BlockSpec(tri.shape, fixed),
                  pl.BlockSpec(lvl.shape, fixed),
                  pl.BlockSpec(sgn.shape, fixed)],
        out_specs=rows,
        out_shape=jax.ShapeDtypeStruct((batch, seq, d_model), BF16),
        scratch_shapes=[pltpu.VMEM((HGRN_HEADS_PER_STEP, HEAD_DIM, HEAD_DIM), F32)],
        compiler_params=_params(("parallel", "parallel", "arbitrary")),
        name="hgrn_recurrence",
    )(qig, qig, qig, log_f, kk, out_norm_gain.reshape(1, HEAD_DIM),
      jnp.asarray(tri, BF16), jnp.asarray(lvl), jnp.asarray(sgn))
    return out.reshape(batch * seq, d_model)


@jax.jit
def kernel(x, positions, norm_gains, w_att_in, w_att_out, w_rec_in, rec_lower_bounds, rec_out_norm,
           w_rec_out, w_ff1, w_ff2):
    batch, seq, d_model = x.shape
    depth = norm_gains.shape[0]
    tables = [_rope_tables(positions, dilation) for _, dilation in DILATED_GROUPS]
    h = x.reshape(batch * seq, d_model)
    for layer in range(depth):
        g = norm_gains[layer]
        j = layer // 2
        if layer % 2 == 0:
            outs, lses = [], []
            for gi, (window, dilation) in enumerate(DILATED_GROUPS):
                assert window // dilation == ATT_BLOCK
                qkv = _att_in_proj(h, g[0], w_att_in, j, gi, dilation, batch, seq, tables[gi])
                o, lse = _attention(qkv, dilation, batch, seq)
                outs.append(o)
                lses.append(lse)
            h = _merge_out_proj(h, g[1], w_att_out, j, outs, lses, batch, seq)
        else:
            qig = _part_proj(h, g[0], w_rec_in, j, (0, 2, 3))
            log_f, kk = _forget_proj(h, g[0], w_rec_in, j, 1, rec_lower_bounds)
            mixed = _hgrn_recurrence(qig.reshape(3, batch, seq, d_model), log_f.reshape(batch, seq, d_model),
                                     kk.reshape(batch, seq, d_model), rec_out_norm[j], batch, seq, d_model)
            h = _out_proj(h, g[1], w_rec_out, j, mixed)
        h = _mlp(h, g[2], g[3], w_ff1, w_ff2, layer)
    return h.reshape(batch, seq, d_model)
```

```python
import functools

import numpy as np
import jax
import jax.numpy as jnp
from jax import lax
from jax.experimental import pallas as pl
from jax.experimental.pallas import tpu as pltpu

F32 = jnp.float32
BF16 = jnp.bfloat16

NORM_EPS = 1e-6
MASK_VALUE = -1e30
LB_FLOOR = 1e-30

DILATED_GROUPS = ((128, 1), (512, 4), (2048, 16))
HEAD_DIM = 128
ATT_HEADS = 8
ATT_BLOCK = 128
ATT_SUBS_MAX = 8
ATT_GROUP_WIDTH = ATT_HEADS * HEAD_DIM
ROPE_THETA = 500000.0
ROPE_DIM = HEAD_DIM // 4
ROPE_HALF = ROPE_DIM // 2
TOKEN_TILE = 256
PROJ_TILE_M, PROJ_SUB = 512, 256
PART_TILE_M = 1024
OUT_TILE, OUT_SUB = 512, 256
MLP_TILE_M, MLP_TILE_F = 1024, 512
HGRN_CHUNK = 128
HGRN_LEVELS = 7
HGRN_HEADS_PER_STEP = 16
HGRN_GROUP = 16
LOG2E = 1.4426950408889634

VMEM_LIMIT_BYTES = 56 * 1024 * 1024
SINGLE = pl.Buffered(1)


def _params(semantics):
    return pltpu.CompilerParams(dimension_semantics=semantics, vmem_limit_bytes=VMEM_LIMIT_BYTES)


def _rms_scale(x, gain):
    ms = jnp.mean(x * x, axis=-1, keepdims=True)
    return x * lax.rsqrt(ms + NORM_EPS) * gain


def _rope_kernel(pos_ref, invf_ref, cos_ref, sa_ref, sb_ref):
    ang = pos_ref[...].astype(F32) * invf_ref[...]
    lane = lax.broadcasted_iota(jnp.int32, ang.shape, 1)
    c = jnp.cos(ang)
    s = jnp.sin(ang)
    cos_ref[...] = jnp.where(lane < ROPE_DIM, c, 1.0)
    sa_ref[...] = jnp.where((lane >= ROPE_HALF) & (lane < ROPE_DIM), s, 0.0)
    sb_ref[...] = jnp.where(lane < ROPE_HALF, -s, 0.0)


def _rope_tables(positions, dilation):
    batch, seq = positions.shape
    t = batch * seq
    tr = 1024
    pos = positions.reshape(batch, seq // dilation, dilation).transpose(0, 2, 1)
    inv_freq = ROPE_THETA ** (-jnp.arange(ROPE_HALF, dtype=F32) / ROPE_HALF)
    invf = jnp.zeros((1, HEAD_DIM), F32).at[0, :ROPE_DIM].set(jnp.tile(inv_freq, 2))
    out = jax.ShapeDtypeStruct((t, HEAD_DIM), F32)
    tabs = pl.pallas_call(
        _rope_kernel,
        grid=(t // tr,),
        in_specs=[pl.BlockSpec((tr, 1), lambda i: (i, 0)),
                  pl.BlockSpec((1, HEAD_DIM), lambda i: (0, 0))],
        out_specs=[pl.BlockSpec((tr, HEAD_DIM), lambda i: (i, 0))] * 3,
        out_shape=[out, out, out],
        compiler_params=_params(("parallel",)),
        name="rope_tables",
    )(pos.reshape(t, 1), invf)
    return [tab.reshape(batch, dilation, seq // dilation, HEAD_DIM) for tab in tabs]


def _att_in_proj_kernel(*refs, dilation):
    refs = list(refs)
    planes_ref = refs.pop() if dilation > 1 else None
    x_ref, g_ref, w_ref, cos_ref, sa_ref, sb_ref, o_ref, wb_ref, xn_ref = refs
    tm, d_model = x_ref.shape
    n = tm // dilation

    _cast_rows_once(w_ref, wb_ref, (pl.program_id(0) == 0) & (pl.program_id(1) == 0))
    xn = _rms_scale(x_ref[...], g_ref[...])
    if dilation == 1:
        xn_ref[...] = xn.astype(BF16)
    else:
        for c in range(d_model // HEAD_DIM):
            planes_ref[c] = xn[:, c * HEAD_DIM:(c + 1) * HEAD_DIM]
        for r in range(dilation):
            for c in range(d_model // HEAD_DIM):
                rows = planes_ref[c, pl.ds(r, n, stride=dilation), :]
                xn_ref[r * n:(r + 1) * n, c * HEAD_DIM:(c + 1) * HEAD_DIM] = rows.astype(BF16)

    acc = jnp.dot(xn_ref[...], wb_ref[...], preferred_element_type=F32)

    def store(lo, val):
        for r in range(dilation):
            o_ref[r, :, lo:lo + val.shape[1]] = val[r * n:(r + 1) * n].astype(o_ref.dtype)

    cos, sa, sb = (ref[...].reshape(tm, HEAD_DIM) for ref in (cos_ref, sa_ref, sb_ref))
    scale = HEAD_DIM ** -0.5
    for part, mult in ((0, scale), (1, 1.0)):
        c, a, b = cos * mult, sa * mult, sb * mult
        for hd in range(ATT_HEADS):
            lo = part * ATT_GROUP_WIDTH + hd * HEAD_DIM
            xh = acc[:, lo:lo + HEAD_DIM]
            rot = (xh * c + pltpu.roll(xh, ROPE_HALF, 1) * a
                   + pltpu.roll(xh, HEAD_DIM - ROPE_HALF, 1) * b)
            store(lo, rot)
    store(2 * ATT_GROUP_WIDTH, acc[:, 2 * ATT_GROUP_WIDTH:])


def _att_in_proj(h, gain, w, layer, group, dilation, batch, seq, tables):
    t, d_model = h.shape
    tm = TOKEN_TILE
    tiles = seq // tm
    n = tm // dilation
    width = 3 * ATT_GROUP_WIDTH
    residue_tile = lambda b, i: (b, 0, i, 0)
    scratch = [pltpu.VMEM((d_model, width), BF16), pltpu.VMEM((tm, d_model), BF16)]
    if dilation > 1:
        scratch.append(pltpu.VMEM((d_model // HEAD_DIM, tm, HEAD_DIM), F32))
    return pl.pallas_call(
        functools.partial(_att_in_proj_kernel, dilation=dilation),
        grid=(batch, tiles),
        in_specs=[pl.BlockSpec((tm, d_model), lambda b, i: (b * tiles + i, 0)),
                  pl.BlockSpec((1, d_model), lambda b, i: (0, 0)),
                  pl.BlockSpec((None, d_model, width), lambda b, i: (layer, 0, group), pipeline_mode=SINGLE)]
                 + [pl.BlockSpec((None, dilation, n, HEAD_DIM), residue_tile)] * 3,
        out_specs=pl.BlockSpec((None, dilation, n, width), residue_tile),
        out_shape=jax.ShapeDtypeStruct((batch, dilation, seq // dilation, width), BF16),
        scratch_shapes=scratch,
        compiler_params=_params(("arbitrary", "arbitrary")),
        name="att_in_proj",
    )(h, gain.reshape(1, d_model), w, *tables)


def _cast_rows_once(w_ref, wb_ref, first):
    @pl.when(first)
    def _():
        def cast_rows(i, carry):
            rows = pl.ds(pl.multiple_of(i * HEAD_DIM, HEAD_DIM), HEAD_DIM)
            wb_ref[rows, :] = w_ref[rows, :].astype(BF16)
            return carry
        lax.fori_loop(0, w_ref.shape[0] // HEAD_DIM, cast_rows, 0)


def _part_proj_kernel(x_ref, g_ref, w_ref, o_ref, wb_ref):
    _cast_rows_once(w_ref, wb_ref, pl.program_id(1) == 0)
    for lo in range(0, x_ref.shape[0], PROJ_SUB):
        rows = slice(lo, lo + PROJ_SUB)
        xn = _rms_scale(x_ref[rows, :], g_ref[...]).astype(BF16)
        o_ref[rows, :] = jnp.dot(xn, wb_ref[...], preferred_element_type=F32).astype(o_ref.dtype)


def _forget_proj_kernel(x_ref, g_ref, w_ref, lbp_ref, logf_ref, kk_ref, wb_ref, *, layer_j):
    _cast_rows_once(w_ref, wb_ref, pl.program_id(0) == 0)
    lbp = lbp_ref[...]
    e = jnp.exp(lbp - jnp.max(lbp, axis=0, keepdims=True))
    p = e / jnp.sum(e, axis=0, keepdims=True)
    csum = p[0:1]
    for t in range(1, layer_j + 1):
        csum = csum + p[t:t + 1]
    lb = csum - p[0:1]
    lb_floor = jnp.maximum(lb, LB_FLOOR)
    one_minus_lb = 1.0 - lb
    for lo in range(0, x_ref.shape[0], PROJ_SUB):
        rows = slice(lo, lo + PROJ_SUB)
        xn = _rms_scale(x_ref[rows, :], g_ref[...]).astype(BF16)
        z = jnp.dot(xn, wb_ref[...], preferred_element_type=F32)
        en = jnp.exp(-jnp.abs(z))
        rcp = 1.0 / (1.0 + en)
        pos = z >= 0.0
        logf_ref[rows, :] = jnp.log(lb_floor + one_minus_lb * (jnp.where(pos, 1.0, en) * rcp))
        kk_ref[rows, :] = (one_minus_lb * (jnp.where(pos, en, 1.0) * rcp) - (lb_floor - lb)).astype(kk_ref.dtype)


def _part_proj(h, gain, w, layer, parts):
    t, d_model = h.shape
    tm = PART_TILE_M

    def part_of(p):
        idx = parts[-1]
        for k in range(len(parts) - 2, -1, -1):
            idx = jnp.where(p == k, parts[k], idx)
        return idx

    return pl.pallas_call(
        _part_proj_kernel,
        grid=(len(parts), t // tm),
        in_specs=[pl.BlockSpec((tm, d_model), lambda p, i: (i, 0)),
                  pl.BlockSpec((1, d_model), lambda p, i: (0, 0)),
                  pl.BlockSpec((None, d_model, d_model), lambda p, i: (layer, 0, part_of(p)),
                               pipeline_mode=SINGLE)],
        out_specs=pl.BlockSpec((None, tm, d_model), lambda p, i: (p, i, 0)),
        out_shape=jax.ShapeDtypeStruct((len(parts), t, d_model), BF16),
        scratch_shapes=[pltpu.VMEM((d_model, d_model), BF16)],
        compiler_params=_params(("arbitrary", "arbitrary")),
        name="part_proj",
    )(h, gain.reshape(1, d_model), w)


def _forget_proj(h, gain, w, layer, part, lower_bound_params):
    t, d_model = h.shape
    tm = PROJ_TILE_M
    n_layers = lower_bound_params.shape[0]
    row = lambda i: (i, 0)
    fixed = lambda i: (0, 0)
    return pl.pallas_call(
        functools.partial(_forget_proj_kernel, layer_j=layer),
        grid=(t // tm,),
        in_specs=[pl.BlockSpec((tm, d_model), row),
                  pl.BlockSpec((1, d_model), fixed),
                  pl.BlockSpec((None, d_model, d_model), lambda i: (layer, 0, part), pipeline_mode=SINGLE),
                  pl.BlockSpec((n_layers, d_model), fixed)],
        out_specs=[pl.BlockSpec((tm, d_model), row), pl.BlockSpec((tm, d_model), row)],
        out_shape=[jax.ShapeDtypeStruct((t, d_model), F32), jax.ShapeDtypeStruct((t, d_model), BF16)],
        scratch_shapes=[pltpu.VMEM((d_model, d_model), BF16)],
        compiler_params=_params(("arbitrary",)),
        name="forget_proj",
    )(h, gain.reshape(1, d_model), w, lower_bound_params)


def _attention_kernel(*refs, has_prev, subs):
    if has_prev:
        q_ref, k_ref, v_ref, o_ref, lse_ref, s_ref, prev_ref = refs
        first_block = pl.program_id(2) == 0

        @pl.when(first_block)
        def _():
            prev_ref[...] = jnp.zeros_like(prev_ref)
    else:
        q_ref, k_ref, v_ref, o_ref, lse_ref, s_ref = refs
    n_keys = s_ref.shape[2]
    row = lax.broadcasted_iota(jnp.int32, (ATT_BLOCK, n_keys), 0)
    col = lax.broadcasted_iota(jnp.int32, (ATT_BLOCK, n_keys), 1)
    if has_prev:
        band = (col >= row) & (col <= row + ATT_BLOCK)
    else:
        band = col <= row
    nt = (((1,), (1,)), ((), ()))
    heads = [slice(hd * HEAD_DIM, (hd + 1) * HEAD_DIM) for hd in range(ATT_HEADS)]
    lane = lax.broadcasted_iota(jnp.int32, (ATT_BLOCK, HEAD_DIM), 1)

    for sub in range(subs):
        rows = slice(sub * ATT_BLOCK, (sub + 1) * ATT_BLOCK)
        if has_prev:
            before = slice((sub - 1) * ATT_BLOCK, sub * ATT_BLOCK)
            mask = band & ((col >= ATT_BLOCK) | jnp.logical_not(first_block)) if sub == 0 else band

            def keys(sl):
                prev = prev_ref[0, :, sl] if sub == 0 else k_ref[before, sl]
                return jnp.concatenate([prev, k_ref[rows, sl]], axis=0)

            def values(sl):
                prev = prev_ref[1, :, sl] if sub == 0 else v_ref[before, sl]
                return jnp.concatenate([prev, v_ref[rows, sl]], axis=0)

            queries = lambda sl: q_ref[rows, sl]
            o_blk, lse_blk = o_ref.at[rows], lse_ref.at[rows]
        else:
            mask = band
            keys = lambda sl: k_ref[sub, :, sl]
            values = lambda sl: v_ref[sub, :, sl]
            queries = lambda sl: q_ref[sub, :, sl]
            o_blk, lse_blk = o_ref.at[sub], lse_ref.at[sub]

        maxes = []
        for hd, sl in enumerate(heads):
            s = lax.dot_general(queries(sl), keys(sl), nt, preferred_element_type=F32)
            s = jnp.where(mask, s, MASK_VALUE)
            s_ref[sub * ATT_HEADS + hd] = s
            maxes.append(jnp.max(s, axis=-1, keepdims=True))
        lse_tile = jnp.zeros((ATT_BLOCK, HEAD_DIM), F32)
        for hd, sl in enumerate(heads):
            p = jnp.exp(s_ref[sub * ATT_HEADS + hd] - maxes[hd])
            l = jnp.sum(p, axis=-1, keepdims=True)
            acc = jnp.dot(p.astype(BF16), values(sl), preferred_element_type=F32)
            o_blk[:, sl] = (acc / l).astype(o_ref.dtype)
            lse_tile = jnp.where(lane == hd, maxes[hd] + jnp.log(l), lse_tile)
        lse_blk[...] = lse_tile
    if has_prev:
        last = slice((subs - 1) * ATT_BLOCK, subs * ATT_BLOCK)
        prev_ref[0] = k_ref[last, :]
        prev_ref[1] = v_ref[last, :]


def _attention(qkv, dilation, batch, seq):
    length = seq // dilation
    nb = length // ATT_BLOCK
    has_prev = nb > 1
    if has_prev:
        subs = min(ATT_SUBS_MAX, nb)
        grid = (batch, dilation, nb // subs)
        shape = lambda w: (None, None, subs * ATT_BLOCK, w)
    else:
        subs = min(ATT_SUBS_MAX, dilation)
        grid = (batch, dilation // subs, nb)
        shape = lambda w: (None, subs, ATT_BLOCK, w)

    def spec(w, part):
        return pl.BlockSpec(shape(w), lambda b, r, i: (b, r, i, part))

    n_keys = 2 * ATT_BLOCK if has_prev else ATT_BLOCK
    scratch = [pltpu.VMEM((subs * ATT_HEADS, ATT_BLOCK, n_keys), F32)]
    if has_prev:
        scratch.append(pltpu.VMEM((2, ATT_BLOCK, ATT_GROUP_WIDTH), BF16))
    return pl.pallas_call(
        functools.partial(_attention_kernel, has_prev=has_prev, subs=subs),
        grid=grid,
        in_specs=[spec(ATT_GROUP_WIDTH, 0), spec(ATT_GROUP_WIDTH, 1), spec(ATT_GROUP_WIDTH, 2)],
        out_specs=[spec(ATT_GROUP_WIDTH, 0), spec(HEAD_DIM, 0)],
        out_shape=[jax.ShapeDtypeStruct((batch, dilation, length, ATT_GROUP_WIDTH), BF16),
                   jax.ShapeDtypeStruct((batch, dilation, length, HEAD_DIM), F32)],
        scratch_shapes=scratch,
        compiler_params=_params(("parallel", "parallel", "arbitrary")),
        name="banded_attention",
    )(qkv, qkv, qkv)


def _cast_weight_once(w_ref, wb_ref, first):
    @pl.when(first)
    def _():
        wb_ref[...] = w_ref[...].astype(BF16)


def _project_norm_residual(a_ref, wb_ref, h_ref, g_ref, out_ref):
    for lo in range(0, a_ref.shape[0], OUT_SUB):
        rows = slice(lo, lo + OUT_SUB)
        y = jnp.dot(a_ref[rows, :], wb_ref[...], preferred_element_type=F32)
        out_ref[rows, :] = h_ref[rows, :] + _rms_scale(y, g_ref[...])


def _merge_out_proj_kernel(o1, o2, o3, l1, l2, l3, h_ref, g_ref, w_ref, out_ref, wb_ref, ot_ref, lt_ref, a_ref):
    _cast_weight_once(w_ref, wb_ref, (pl.program_id(0) == 0) & (pl.program_id(1) == 0))
    tm = h_ref.shape[0]
    for gi, (o_g, l_g) in enumerate(((o2, l2), (o3, l3))):
        dilation = o_g.shape[0]
        n = tm // dilation
        for r in range(dilation):
            rows = pl.ds(r, n, stride=dilation)
            lt_ref[gi, rows, :] = l_g[r]
            for hd in range(ATT_HEADS):
                ot_ref[gi, hd, rows, :] = o_g[r, :, hd * HEAD_DIM:(hd + 1) * HEAD_DIM].astype(F32)
    lses = (l1[0], lt_ref[0], lt_ref[1])
    m = jnp.maximum(jnp.maximum(lses[0], lses[1]), lses[2])
    es = [jnp.exp(v - m) for v in lses]
    den = es[0] + es[1] + es[2]
    ws = [e / den for e in es]
    for hd in range(ATT_HEADS):
        sl = slice(hd * HEAD_DIM, (hd + 1) * HEAD_DIM)
        acc = ws[0][:, hd:hd + 1] * o1[0, :, sl].astype(F32)
        acc = acc + ws[1][:, hd:hd + 1] * ot_ref[0, hd]
        acc = acc + ws[2][:, hd:hd + 1] * ot_ref[1, hd]
        a_ref[:, sl] = acc.astype(BF16)
    _project_norm_residual(a_ref, wb_ref, h_ref, g_ref, out_ref)


def _merge_out_proj(h, gain, w, layer, outs, lses, batch, seq):
    t, d_model = h.shape
    k = w.shape[1]
    tm = OUT_TILE
    tiles = seq // tm
    token_tile = lambda b, i: (b * tiles + i, 0)
    fixed = lambda b, i: (0, 0)

    def group_spec(arr):
        dilation, width = arr.shape[1], arr.shape[3]
        return pl.BlockSpec((None, dilation, tm // dilation, width), lambda b, i: (b, 0, i, 0))

    in_specs = [group_spec(a) for a in outs] + [group_spec(a) for a in lses]
    in_specs += [pl.BlockSpec((tm, d_model), token_tile), pl.BlockSpec((1, d_model), fixed),
                 pl.BlockSpec((None, k, d_model), lambda b, i: (layer, 0, 0), pipeline_mode=SINGLE)]
    return pl.pallas_call(
        _merge_out_proj_kernel,
        grid=(batch, tiles),
        in_specs=in_specs,
        out_specs=pl.BlockSpec((tm, d_model), token_tile),
        out_shape=jax.ShapeDtypeStruct((t, d_model), F32),
        scratch_shapes=[pltpu.VMEM((k, d_model), BF16),
                        pltpu.VMEM((2, ATT_HEADS, tm, HEAD_DIM), F32), pltpu.VMEM((2, tm, HEAD_DIM), F32),
                        pltpu.VMEM((tm, k), BF16)],
        compiler_params=_params(("arbitrary", "arbitrary")),
        name="merge_out_proj",
    )(*outs, *lses, h, gain.reshape(1, d_model), w)


def _out_proj_kernel(a_ref, h_ref, g_ref, w_ref, out_ref, wb_ref):
    _cast_weight_once(w_ref, wb_ref, pl.program_id(0) == 0)
    _project_norm_residual(a_ref, wb_ref, h_ref, g_ref, out_ref)


def _out_proj(h, gain, w, layer, mixed):
    t, d_model = h.shape
    k = w.shape[1]
    tm = OUT_TILE
    row = lambda i: (i, 0)
    fixed = lambda i: (0, 0)
    return pl.pallas_call(
        _out_proj_kernel,
        grid=(t // tm,),
        in_specs=[pl.BlockSpec((tm, k), row), pl.BlockSpec((tm, d_model), row),
                  pl.BlockSpec((1, d_model), fixed),
                  pl.BlockSpec((None, k, d_model), lambda i: (layer, 0, 0), pipeline_mode=SINGLE)],
        out_specs=pl.BlockSpec((tm, d_model), row),
        out_shape=jax.ShapeDtypeStruct((t, d_model), F32),
        scratch_shapes=[pltpu.VMEM((k, d_model), BF16)],
        compiler_params=_params(("arbitrary",)),
        name="out_proj",
    )(mixed, h, gain.reshape(1, d_model), w)


def _mlp_kernel(h_ref, g_in_ref, g_out_ref, w1_ref, w2_ref, out_ref, un_ref):
    f = pl.program_id(1)

    @pl.when(f == 0)
    def _():
        un_ref[...] = _rms_scale(h_ref[...], g_in_ref[...]).astype(BF16)
        out_ref[...] = jnp.zeros_like(out_ref)

    a = jnp.dot(un_ref[...], w1_ref[...].astype(BF16), preferred_element_type=F32)
    a = jnp.square(jnp.maximum(a, 0.0))
    out_ref[...] += jnp.dot(a.astype(BF16), w2_ref[...].astype(BF16), preferred_element_type=F32)

    @pl.when(f == pl.num_programs(1) - 1)
    def _():
        out_ref[...] = h_ref[...] + _rms_scale(out_ref[...], g_out_ref[...])


def _mlp(h, g_in, g_out, w1, w2, layer):
    t, d_model = h.shape
    d_ff = w1.shape[2]
    tm, tf = MLP_TILE_M, MLP_TILE_F
    return pl.pallas_call(
        _mlp_kernel,
        grid=(t // tm, d_ff // tf),
        in_specs=[pl.BlockSpec((tm, d_model), lambda m, f: (m, 0), pipeline_mode=SINGLE),
                  pl.BlockSpec((1, d_model), lambda m, f: (0, 0)),
                  pl.BlockSpec((1, d_model), lambda m, f: (0, 0)),
                  pl.BlockSpec((None, d_model, tf), lambda m, f: (layer, 0, f)),
                  pl.BlockSpec((None, tf, d_model), lambda m, f: (layer, f, 0))],
        out_specs=pl.BlockSpec((tm, d_model), lambda m, f: (m, 0)),
        out_shape=jax.ShapeDtypeStruct((t, d_model), F32),
        scratch_shapes=[pltpu.VMEM((tm, d_model), BF16)],
        compiler_params=_params(("parallel", "arbitrary")),
        name="mlp",
    )(h, g_in.reshape(1, d_model), g_out.reshape(1, d_model), w1, w2)


def _hgrn_constants():
    r = np.arange(HGRN_CHUNK)[:, None]
    c = np.arange(HGRN_CHUNK)[None, :]
    tri = (c <= r).astype(np.float32)
    msb = np.floor(np.log2(np.maximum(r ^ c, 1))).astype(np.int32)
    lvl = np.where(c < r, msb, -1).astype(np.int32)
    sgn = np.concatenate([np.where((r >> j) & 1 == 1, LOG2E, -LOG2E) * np.ones_like(c)
                          for j in range(2, HGRN_LEVELS)], axis=0).astype(np.float32)
    return tri, lvl, sgn


def _hgrn_kernel(q_ref, i_ref, gt_ref, logf_ref, kk_ref, gn_ref, tri_ref, lvl_ref, sgn_ref, o_ref, st_ref):
    nh = HGRN_HEADS_PER_STEP
    cs = HGRN_CHUNK

    @pl.when(pl.program_id(2) == 0)
    def _():
        st_ref[...] = jnp.zeros_like(st_ref)

    tri = tri_ref[...]
    lvl = lvl_ref[...]
    level_mask = [lvl == j for j in range(HGRN_LEVELS)]
    gw = HGRN_GROUP * HEAD_DIM
    row = lax.broadcasted_iota(jnp.int32, (cs, gw), 0)
    r4 = row & 3
    nt = (((1,), (1,)), ((), ()))
    tn = (((0,), (0,)), ((), ()))
    for grp in range(nh // HGRN_GROUP):
        gsl = slice(grp * gw, (grp + 1) * gw)
        log_f = logf_ref[:, gsl]

        g_hi = log_f.astype(BF16)
        r1 = log_f - g_hi.astype(F32)
        g_mid = r1.astype(BF16)
        g_lo = (r1 - g_mid.astype(F32)).astype(BF16)
        b = (jnp.dot(tri, g_hi, preferred_element_type=F32)
             + jnp.dot(tri, g_mid, preferred_element_type=F32)
             + jnp.dot(tri, g_lo, preferred_element_type=F32))
        b_last = b[cs - 1:cs, :]

        up = pltpu.roll(log_f, cs - 1, 0)
        down = pltpu.roll(log_f, 1, 0)
        level_e = [
            jnp.exp(jnp.where((row & 1) == 1, log_f, 0.0)),
            jnp.exp(jnp.where(r4 == 0, up, jnp.where(r4 == 1, 0.0, jnp.where(r4 == 2, log_f, log_f + down)))),
        ]
        for j in range(2, HGRN_LEVELS):
            half = 1 << j
            nblk = cs // (2 * half)
            b3 = b.reshape(nblk, 2 * half, gw)
            mid = jnp.broadcast_to(b3[:, half - 1:half, :], b3.shape).reshape(cs, gw)
            sign_log2e = jnp.tile(sgn_ref[(j - 2) * cs:(j - 1) * cs, :], (1, HGRN_GROUP))
            level_e.append(jnp.exp2((b - mid) * sign_log2e))
        e_incl = jnp.exp(b)
        e_suffix = jnp.exp(b_last - b)

        for hh in range(HGRN_GROUP):
            head = grp * HGRN_GROUP + hh
            sl = slice(head * HEAD_DIM, (head + 1) * HEAD_DIM)
            loc = slice(hh * HEAD_DIM, (hh + 1) * HEAD_DIM)
            qb = q_ref[:, sl]
            qh = qb.astype(F32)
            kb = kk_ref[:, sl]
            kh = kb.astype(F32)
            vb = i_ref[:, sl]
            vh = vb.astype(F32)
            a_mat = jnp.zeros((cs, cs), F32)
            for j in range(HGRN_LEVELS):
                ej = level_e[j][:, loc].astype(BF16)
                aj = lax.dot_general(qb * ej, kb * ej, nt, preferred_element_type=F32)
                a_mat = jnp.where(level_mask[j], aj, a_mat)
            q_dec = qb * e_incl[:, loc].astype(BF16)
            k_dec = kb * e_suffix[:, loc].astype(BF16)
            st = st_ref[head]
            inter = lax.dot_general(q_dec, st.astype(BF16), nt, preferred_element_type=F32)
            intra = jnp.dot(a_mat.astype(BF16), vb, preferred_element_type=F32)
            diag = jnp.sum(qh * kh, axis=-1, keepdims=True) * vh
            o = inter + intra + diag
            st_ref[head] = (st * e_incl[cs - 1:cs, loc]
                            + lax.dot_general(vb, k_dec, tn, preferred_element_type=F32))
            gt = gt_ref[:, sl].astype(F32)
            on = _rms_scale(o, gn_ref[...]) * (gt * (1.0 / (1.0 + jnp.exp(-gt))))
            o_ref[:, sl] = on.astype(o_ref.dtype)


def _hgrn_recurrence(qig, log_f, kk, out_norm_gain, batch, seq, d_model):
    wb = HGRN_HEADS_PER_STEP * HEAD_DIM
    hb = d_model // wb
    tri, lvl, sgn = _hgrn_constants()

    def part(k):
        return pl.BlockSpec((None, None, HGRN_CHUNK, wb), lambda b, g, c: (k, b, c, g))

    rows = pl.BlockSpec((None, HGRN_CHUNK, wb), lambda b, g, c: (b, c, g))

    fixed = lambda b, g, c: (0, 0)
    out = pl.pallas_call(
        _hgrn_kernel,
        grid=(batch, hb, seq // HGRN_CHUNK),
        in_specs=[part(0), part(1), part(2), rows, rows,
                  pl.BlockSpec((1, HEAD_DIM), fixed),
                  pl.BlockSpec(tri.shape, fixed),
                  pl.BlockSpec(lvl.shape, fixed),
                  pl.BlockSpec(sgn.shape, fixed)],
        out_specs=rows,
        out_shape=jax.ShapeDtypeStruct((batch, seq, d_model), BF16),
        scratch_shapes=[pltpu.VMEM((HGRN_HEADS_PER_STEP, HEAD_DIM, HEAD_DIM), F32)],
        compiler_params=_params(("parallel", "parallel", "arbitrary")),
        name="hgrn_recurrence",
    )(qig, qig, qig, log_f, kk, out_norm_gain.reshape(1, HEAD_DIM),
      jnp.asarray(tri, BF16), jnp.asarray(lvl), jnp.asarray(sgn))
    return out.reshape(batch * seq, d_model)


@jax.jit
def kernel(x, positions, norm_gains, w_att_in, w_att_out, w_rec_in, rec_lower_bounds, rec_out_norm,
           w_rec_out, w_ff1, w_ff2):
    batch, seq, d_model = x.shape
    depth = norm_gains.shape[0]
    tables = [_rope_tables(positions, dilation) for _, dilation in DILATED_GROUPS]
    h = x.reshape(batch * seq, d_model)
    for layer in range(depth):
        g = norm_gains[layer]
        j = layer // 2
        if layer % 2 == 0:
            outs, lses = [], []
            for gi, (window, dilation) in enumerate(DILATED_GROUPS):
                assert window // dilation == ATT_BLOCK
                qkv = _att_in_proj(h, g[0], w_att_in, j, gi, dilation, batch, seq, tables[gi])
                o, lse = _attention(qkv, dilation, batch, seq)
                outs.append(o)
                lses.append(lse)
            h = _merge_out_proj(h, g[1], w_att_out, j, outs, lses, batch, seq)
        else:
            qig = _part_proj(h, g[0], w_rec_in, j, (0, 2, 3))
            log_f, kk = _forget_proj(h, g[0], w_rec_in, j, 1, rec_lower_bounds)
            mixed = _hgrn_recurrence(qig.reshape(3, batch, seq, d_model), log_f.reshape(batch, seq, d_model),
                                     kk.reshape(batch, seq, d_model), rec_out_norm[j], batch, seq, d_model)
            h = _out_proj(h, g[1], w_rec_out, j, mixed)
        h = _mlp(h, g[2], g[3], w_ff1, w_ff2, layer)
    return h.reshape(batch, seq, d_model)
```
